```python
import math
import jax
import jax.numpy as jnp
from jax import lax
import numpy as np

D_MODEL = 4096
BATCH = 32
SEQ = 256
DEPTH = 2
DEC_BATCH = 4
DEC_SEQ = 2048
PAST_LEN = 256

GRID_W = 64
MIX = D_MODEL
GROUP_W = MIX // 4
HD_A = 64
H_A = GROUP_W // HD_A
R_DECAY = 64
R_AICL = 64
R_GATE = 160
GN_EPS = 64e-5
DK_B = 64
DV_B = 2 * DK_B
H_B = GROUP_W // DV_B
HD_C = 64
H_C = GROUP_W // HD_C
WIN_R = 8
WIN_C = 16
HY_W = GROUP_W
HY_ORDER = 2
HY_SHORT = 3
HY_BANDS = 16
HY_EMB = 1 + 2 * HY_BANDS
HY_HID = 64
HY_TARGET = 1e-2
HY_FAST = 0.3
HY_SLOW = 1.5
HY_MIN_DECAY = math.log(HY_TARGET) / HY_SLOW
HY_MAX_DECAY = math.log(HY_TARGET) / HY_FAST
D_FF = 4 * D_MODEL
N_MOD = 6
Q_BLOCK = 128
ROPE_BASE = 10000.0
EPS = 1e-6
IN_SIZES = (GROUP_W, GROUP_W, GROUP_W, R_DECAY, R_DECAY, R_AICL, R_AICL, R_GATE,
            GROUP_W, GROUP_W, GROUP_W, GROUP_W, GROUP_W, GROUP_W, (HY_ORDER + 1) * HY_W)
N_IN = sum(IN_SIZES)

kernel_name = 'hybrid_diffusion_parallel_heads_step'


def _rmsnorm(x, g):
    xf = x.astype(jnp.float32)
    y = xf * lax.rsqrt(jnp.mean(xf * xf, axis=-1, keepdims=True) + EPS)
    return (y * g.astype(jnp.float32)).astype(x.dtype)


def _modulation(cvec, w_mod, b_mod):
    return jnp.split(jax.nn.silu(cvec) @ w_mod + b_mod, N_MOD, axis=-1)


def _axial_rope(x):
    T = x.shape[-2]
    t = jnp.arange(T)
    row = (t // GRID_W).astype(jnp.float32)
    col = (t % GRID_W).astype(jnp.float32)
    half = x.shape[-1] // 2
    nf = half // 2
    inv = ROPE_BASE ** (-jnp.arange(nf, dtype=jnp.float32) / nf)

    def rot(xp, pos):
        ang = pos[:, None] * inv[None, :]
        cos = jnp.cos(ang).astype(x.dtype)
        sin = jnp.sin(ang).astype(x.dtype)
        x1, x2 = xp[..., :nf], xp[..., nf:]
        return jnp.concatenate([x1 * cos - x2 * sin, x1 * sin + x2 * cos], axis=-1)

    return jnp.concatenate([rot(x[..., :half], row), rot(x[..., half:], col)], axis=-1)


def _query_blocks(q):
    B, H, T, d = q.shape
    return jnp.moveaxis(q.reshape(B, H, T // Q_BLOCK, Q_BLOCK, d), 2, 0)


def _merge_blocks(o):
    nb, B, H, QB, d = o.shape
    return jnp.moveaxis(o, 0, 2).reshape(B, H, nb * QB, d)


def _dense_attention(q, k, v):
    def blk(qb):
        p = jax.nn.softmax(jnp.einsum('bhqd,bhkd->bhqk', qb, k).astype(jnp.float32), axis=-1)
        return jnp.einsum('bhqk,bhkd->bhqd', p.astype(v.dtype), v)
    return _merge_blocks(lax.map(blk, _query_blocks(q)))


def _diff_attention(q1, q2, k1, k2, v, lam):
    def blk(qs):
        qa, qb = qs
        pa = jax.nn.softmax(jnp.einsum('bhqd,bhkd->bhqk', qa, k1).astype(jnp.float32), axis=-1)
        pb = jax.nn.softmax(jnp.einsum('bhqd,bhkd->bhqk', qb, k2).astype(jnp.float32), axis=-1)
        return jnp.einsum('bhqk,bhkd->bhqd', (pa - lam * pb).astype(v.dtype), v)
    return _merge_blocks(lax.map(blk, (_query_blocks(q1), _query_blocks(q2))))


def _wkv7_scan(s0, r, w, k, v, kk, a, reverse):
    xs = tuple(jnp.moveaxis(t, 1, 0) for t in (r, w, k, v, kk, a))

    def step(s, inp):
        r_t, w_t, k_t, v_t, kk_t, a_t = inp
        sa = jnp.einsum('bhvk,bhk->bhv', s, -kk_t)
        s = (s * w_t[:, :, None, :] + sa[..., None] * (kk_t * a_t)[:, :, None, :]
             + v_t[..., None] * k_t[:, :, None, :])
        return s, jnp.einsum('bhvk,bhk->bhv', s, r_t)

    s, o = lax.scan(step, s0.astype(jnp.float32), xs, reverse=reverse)
    return s, jnp.moveaxis(o, 0, 1)


def _rwkv_mixer(r, k, v, w_down, a_down, g_down, p, s0_fwd, s0_bwd):
    B, T, _ = r.shape
    dtype = r.dtype
    f32 = jnp.float32
    heads = lambda t: t.reshape(B, T, H_A, HD_A)
    r, k, v = r.astype(f32), k.astype(f32), v.astype(f32)
    kk = heads(k * p['rwkv_k_k'])
    kk = kk / jnp.maximum(jnp.sqrt(jnp.sum(kk * kk, axis=-1, keepdims=True)), 1e-12)
    outs, states, kds = [], [], []
    for d in range(2):
        wlog = -jax.nn.softplus(-(p['rwkv_w0'][d] + jnp.tanh(w_down[d]) @ p['rwkv_w2'][d]).astype(f32)) - 0.5
        decay = jnp.exp(-jnp.exp(wlog))
        a = jax.nn.sigmoid((p['rwkv_a0'][d] + a_down[d] @ p['rwkv_a2'][d]).astype(f32))
        kd = k * (1.0 + (a - 1.0) * p['rwkv_k_a'])
        s0 = s0_fwd if d == 0 else s0_bwd
        s, o = _wkv7_scan(s0, heads(r), heads(decay), heads(kd), heads(v), kk, heads(a), reverse=(d == 1))
        outs.append(o)
        states.append(s)
        kds.append(kd)
    o = outs[0] + outs[1]
    mu = jnp.mean(o, axis=-1, keepdims=True)
    var = jnp.mean(jnp.square(o - mu), axis=-1, keepdims=True)
    o = ((o - mu) * lax.rsqrt(var + GN_EPS) * p['rwkv_ln_w'].reshape(H_A, HD_A)
         + p['rwkv_ln_b'].reshape(H_A, HD_A))
    bonus = jnp.sum(heads(r) * heads(kds[0] + kds[1]) * p['rwkv_r_k'].reshape(H_A, HD_A),
                    axis=-1, keepdims=True) * heads(v)
    g = jax.nn.sigmoid(g_down) @ p['rwkv_g2']
    y = ((o + bonus).reshape(B, T, GROUP_W) * g).astype(dtype)
    return y, states[0], states[1]


def _diff_mixer(q, k, v, lam_vecs, subln_g, lam_init, ctx=None):
    B, T, _ = q.shape
    q = q.reshape(B, T, H_B, 2, DK_B).transpose(3, 0, 2, 1, 4)
    k = k.reshape(B, T, H_B, 2, DK_B).transpose(3, 0, 2, 1, 4)
    v = v.reshape(B, T, H_B, DV_B).transpose(0, 2, 1, 3)
    q1, q2, k1, k2 = q[0], q[1], k[0], k[1]
    if ctx is None:
        new = (jnp.concatenate([k1, k2], axis=-1), v)
    else:
        q1, q2, k1, k2 = _axial_rope(q1), _axial_rope(q2), _axial_rope(k1), _axial_rope(k2)
        ck, cv = ctx
        k1 = jnp.concatenate([ck[..., :DK_B], k1], axis=2)
        k2 = jnp.concatenate([ck[..., DK_B:], k2], axis=2)
        v = jnp.concatenate([cv, v], axis=2)
        new = None
    scale = DK_B ** -0.5
    lf = lam_vecs.astype(jnp.float32)
    lam = jnp.exp(jnp.sum(lf[0] * lf[1])) - jnp.exp(jnp.sum(lf[2] * lf[3])) + lam_init
    o = _diff_attention(q1 * scale, q2 * scale, k1, k2, v, lam)
    o = _rmsnorm(o, subln_g) * (1.0 - lam_init)
    return o.transpose(0, 2, 1, 3).reshape(B, T, GROUP_W), new


def _neighbourhood_attention(q, k, v, rpb, ck, cv):
    B, H, T, d = q.shape
    rows = T // GRID_W
    kr = min(WIN_R, rows)
    kc = WIN_C
    qg = q.reshape(B, H, rows, GRID_W, d)
    kg = k.reshape(B, H, rows, GRID_W, d)
    vg = v.reshape(B, H, rows, GRID_W, d)
    cols = jnp.arange(GRID_W)
    col_idx = jnp.clip(cols - kc // 2, 0, GRID_W - kc)[:, None] + jnp.arange(kc)[None, :]
    dc = col_idx - cols[:, None] + (WIN_C - 1)

    def row_step(args):
        q_row, r = args
        r0 = jnp.clip(r - kr // 2, 0, rows - kr)
        dr = r0 + jnp.arange(kr) - r + (WIN_R - 1)
        bias = rpb[:, dr[None, :, None], dc[:, None, :]].reshape(H, GRID_W, kr * kc)
        kb = lax.dynamic_slice_in_dim(kg, r0, kr, axis=2)[:, :, :, col_idx]
        vb = lax.dynamic_slice_in_dim(vg, r0, kr, axis=2)[:, :, :, col_idx]
        s_win = (jnp.einsum('bhwd,bhrwcd->bhwrc', q_row, kb).reshape(B, H, GRID_W, kr * kc).astype(jnp.float32)
                 + bias.astype(jnp.float32))
        s_ctx = jnp.einsum('bhwd,bhkd->bhwk', q_row, ck).astype(jnp.float32)
        pr = jax.nn.softmax(jnp.concatenate([s_win, s_ctx], axis=-1), axis=-1).astype(v.dtype)
        p_win = pr[..., :kr * kc].reshape(B, H, GRID_W, kr, kc)
        p_ctx = pr[..., kr * kc:]
        return (jnp.einsum('bhwrc,bhrwcd->bhwd', p_win, vb)
                + jnp.einsum('bhwk,bhkd->bhwd', p_ctx, cv))

    o = lax.map(row_step, (jnp.moveaxis(qg, 2, 0), jnp.arange(rows)))
    return jnp.moveaxis(o, 0, 2).reshape(B, H, T, d)


def _na_mixer(q, k, v, rpb, ctx=None):
    B, T, _ = q.shape
    heads = lambda t: t.reshape(B, T, H_C, HD_C).transpose(0, 2, 1, 3)
    q, k, v = heads(q) * (HD_C ** -0.5), heads(k), heads(v)
    if ctx is None:
        o = _dense_attention(q, k, v)
        new = (k, v)
    else:
        o = _neighbourhood_attention(q, k, v, rpb, ctx[0], ctx[1])
        new = None
    return o.transpose(0, 2, 1, 3).reshape(B, T, GROUP_W), new


def _short_conv(u, w, b):
    C = u.shape[-1]
    y = lax.conv_general_dilated(u, w[:, None, :], window_strides=(1,),
                                 padding=((HY_SHORT // 2, HY_SHORT // 2),),
                                 dimension_numbers=('NWC', 'WIO', 'NWC'),
                                 feature_group_count=C)
    return y + b


def _hyena_filters(L, w1, b1, f1, w2, b2, f2, w3):
    f32 = jnp.float32
    t = jnp.linspace(0.0, 1.0, L, dtype=f32)[:, None]
    ang = 2.0 * math.pi * jnp.arange(L, dtype=f32) / L
    bands = jnp.linspace(1e-4, HY_BANDS - 1, HY_BANDS, dtype=f32)
    z = jnp.concatenate([t, jnp.cos(ang[:, None] * bands[None, :]),
                         -jnp.sin(ang[:, None] * bands[None, :])], axis=-1).astype(w1.dtype)
    hid = jnp.sin(f1 * (z @ w1 + b1))
    hid = jnp.sin(f2 * (hid @ w2 + b2))
    h = (hid @ w3).reshape(L, HY_ORDER, 2, HY_W)
    deltas = jnp.abs(jnp.linspace(HY_MIN_DECAY, HY_MAX_DECAY, HY_W, dtype=f32))
    decay = jnp.exp(-t * deltas[None, :])
    return h * decay[:, None, None, :].astype(h.dtype)


def _bidir_long_conv(u, hf, hb):
    L, C = u.shape[1], u.shape[-1]
    kfull = jnp.concatenate([hf, jnp.zeros((1, C), hf.dtype), hb[:0:-1]], axis=0).astype(jnp.float32)
    U = jnp.fft.rfft(u.astype(jnp.float32), n=2 * L, axis=1)
    K = jnp.fft.rfft(kfull, n=2 * L, axis=0)
    y = jnp.fft.irfft(U * K[None], n=2 * L, axis=1)[:, :L]
    return y.astype(u.dtype)


def _hyena_mixer(u, p):
    u = _short_conv(u, p['hy_sconv_w'], p['hy_sconv_b'])
    v, x1, x2 = jnp.split(u, HY_ORDER + 1, axis=-1)
    h = _hyena_filters(u.shape[1], p['hy_f_w1'], p['hy_f_b1'], p['hy_f_freq1'],
                       p['hy_f_w2'], p['hy_f_b2'], p['hy_f_freq2'], p['hy_f_w3'])
    z = v
    for o, gate in enumerate((x1, x2)):
        z = gate * (_bidir_long_conv(z, h[:, o, 0], h[:, o, 1]) + p['hy_skip'][o] * z)
    return z


def _layer(x, mod, p, lam_init, ctx=None):
    sh1, sc1, ga1, sh2, sc2, ga2 = mod
    h = _rmsnorm(x, p['norm1_g']) * (1.0 + sc1) + sh1
    pieces = jnp.split(h @ p['w_in'], np.cumsum(IN_SIZES)[:-1].tolist(), axis=-1)
    r, k, v, wfd, wbd, afd, abd, gd, qB, kB, vB, qC, kC, vC, uD = pieces
    if ctx is None:
        s0 = jnp.zeros((x.shape[0], H_A, HD_A, HD_A), jnp.float32)
        s0f, s0b, ctx_b, ctx_c = s0, s0, None, None
    else:
        dk, dv, nk, nv, s0f, s0b = ctx
        ctx_b, ctx_c = (dk, dv), (nk, nv)
    yA, sf, sb = _rwkv_mixer(r, k, v, (wfd, wbd), (afd, abd), gd, p, s0f, s0b)
    yB, newB = _diff_mixer(qB, kB, vB, p['diff_lam'], p['diff_subln_g'], lam_init, ctx_b)
    yC, newC = _na_mixer(qC, kC, vC, p['na_rpb'], ctx_c)
    yD = _hyena_mixer(uD, p)
    mix = jnp.concatenate([yA, yB, yC, yD], axis=-1) @ p['w_out']
    x = x + ga1 * mix
    h = _rmsnorm(x, p['norm2_g']) * (1.0 + sc2) + sh2
    x = x + ga2 * (jnp.square(jax.nn.relu(h @ p['w_up'])) @ p['w_down'])
    new = (newB[0], newB[1], newC[0], newC[1], sf, sb) if ctx is None else None
    return x, new


def setup_inputs(seed: int = 0) -> dict:
    key = jax.random.key(seed)
    ks = iter(jax.random.split(key, 64))
    nrm = lambda shape, s: jax.random.normal(next(ks), shape, jnp.float32) * s
    D, L = D_MODEL, DEPTH
    return {
        'x_prompt': nrm((BATCH, SEQ, D), 1.0),
        'x_sample': nrm((DEC_BATCH, DEC_SEQ, D), 1.0),
        'cache_diff_k': nrm((DEC_BATCH, L, H_B, PAST_LEN, 2 * DK_B), 1.0),
        'cache_diff_v': nrm((DEC_BATCH, L, H_B, PAST_LEN, DV_B), 1.0),
        'cache_na_k': nrm((DEC_BATCH, L, H_C, PAST_LEN, HD_C), 1.0),
        'cache_na_v': nrm((DEC_BATCH, L, H_C, PAST_LEN, HD_C), 1.0),
        'state_rwkv_fwd': nrm((DEC_BATCH, L, H_A, HD_A, HD_A), 0.3),
        'state_rwkv_bwd': nrm((DEC_BATCH, L, H_A, HD_A, HD_A), 0.3),
        'c': nrm((DEC_BATCH, D), 1.0),
        'c_ctx': nrm((D,), 1.0),
        'norm1_g': 1.0 + nrm((L, D), 0.05),
        'norm2_g': 1.0 + nrm((L, D), 0.05),
        'w_mod': nrm((L, D, N_MOD * D), 0.5 * D ** -0.5),
        'b_mod': nrm((L, N_MOD * D), 0.01),
        'w_in': nrm((L, D, N_IN), D ** -0.5),
        'rwkv_w0': -2.0 + nrm((L, 2, GROUP_W), 1.0),
        'rwkv_w2': nrm((L, 2, R_DECAY, GROUP_W), 0.5 * R_DECAY ** -0.5),
        'rwkv_a0': nrm((L, 2, GROUP_W), 0.5),
        'rwkv_a2': nrm((L, 2, R_AICL, GROUP_W), 0.5 * R_AICL ** -0.5),
        'rwkv_g2': nrm((L, R_GATE, GROUP_W), R_GATE ** -0.5),
        'rwkv_k_k': 0.85 + nrm((L, GROUP_W), 0.05),
        'rwkv_k_a': 1.0 + nrm((L, GROUP_W), 0.05),
        'rwkv_r_k': nrm((L, GROUP_W), 0.1),
        'rwkv_ln_w': 1.0 + nrm((L, GROUP_W), 0.05),
        'rwkv_ln_b': nrm((L, GROUP_W), 0.01),
        'diff_lam': nrm((L, 4, DK_B), 0.1),
        'diff_subln_g': 1.0 + nrm((L, DV_B), 0.05),
        'na_rpb': nrm((L, H_C, 2 * WIN_R - 1, 2 * WIN_C - 1), 0.1),
        'hy_sconv_w': nrm((L, HY_SHORT, (HY_ORDER + 1) * HY_W), HY_SHORT ** -0.5),
        'hy_sconv_b': nrm((L, (HY_ORDER + 1) * HY_W), 0.01),
        'hy_f_w1': nrm((L, HY_EMB, HY_HID), HY_EMB ** -0.5),
        'hy_f_b1': nrm((L, HY_HID), 0.1),
        'hy_f_freq1': 1.0 + nrm((L, HY_HID), 0.1),
        'hy_f_w2': nrm((L, HY_HID, HY_HID), HY_HID ** -0.5),
        'hy_f_b2': nrm((L, HY_HID), 0.1),
        'hy_f_freq2': 1.0 + nrm((L, HY_HID), 0.1),
        'hy_f_w3': nrm((L, HY_HID, HY_ORDER * 2 * HY_W), 0.05 * HY_HID ** -0.5),
        'hy_skip': nrm((L, HY_ORDER, HY_W), 1.0),
        'w_out': nrm((L, MIX, D), MIX ** -0.5),
        'w_up': nrm((L, D, D_FF), D ** -0.5),
        'w_down': nrm((L, D_FF, D), D_FF ** -0.5),
        'final_g': 1.0 + nrm((D,), 0.05),
    }


def reference(x_prompt, x_sample, cache_diff_k, cache_diff_v, cache_na_k, cache_na_v,
              state_rwkv_fwd, state_rwkv_bwd, c, c_ctx, norm1_g, norm2_g, w_mod, b_mod, w_in,
              rwkv_w0, rwkv_w2, rwkv_a0, rwkv_a2, rwkv_g2, rwkv_k_k, rwkv_k_a, rwkv_r_k,
              rwkv_ln_w, rwkv_ln_b, diff_lam, diff_subln_g, na_rpb, hy_sconv_w, hy_sconv_b,
              hy_f_w1, hy_f_b1, hy_f_freq1, hy_f_w2, hy_f_b2, hy_f_freq2, hy_f_w3, hy_skip,
              w_out, w_up, w_down, final_g):
    xp, xs = x_prompt, x_sample
    new_lists = ([], [], [], [], [], [])
    for l in range(DEPTH):
        p = {
            'norm1_g': norm1_g[l], 'norm2_g': norm2_g[l], 'w_mod': w_mod[l], 'b_mod': b_mod[l],
            'w_in': w_in[l], 'rwkv_w0': rwkv_w0[l], 'rwkv_w2': rwkv_w2[l], 'rwkv_a0': rwkv_a0[l],
            'rwkv_a2': rwkv_a2[l], 'rwkv_g2': rwkv_g2[l], 'rwkv_k_k': rwkv_k_k[l],
            'rwkv_k_a': rwkv_k_a[l], 'rwkv_r_k': rwkv_r_k[l], 'rwkv_ln_w': rwkv_ln_w[l],
            'rwkv_ln_b': rwkv_ln_b[l], 'diff_lam': diff_lam[l], 'diff_subln_g': diff_subln_g[l],
            'na_rpb': na_rpb[l], 'hy_sconv_w': hy_sconv_w[l], 'hy_sconv_b': hy_sconv_b[l],
            'hy_f_w1': hy_f_w1[l], 'hy_f_b1': hy_f_b1[l], 'hy_f_freq1': hy_f_freq1[l],
            'hy_f_w2': hy_f_w2[l], 'hy_f_b2': hy_f_b2[l], 'hy_f_freq2': hy_f_freq2[l],
            'hy_f_w3': hy_f_w3[l], 'hy_skip': hy_skip[l], 'w_out': w_out[l],
            'w_up': w_up[l], 'w_down': w_down[l],
        }
        lam_init = 0.8 - 0.6 * math.exp(-0.3 * l)
        mod_ctx = _modulation(c_ctx, p['w_mod'], p['b_mod'])
        mod_lat = [m[:, None, :] for m in _modulation(c, p['w_mod'], p['b_mod'])]
        xp, ctx_new = _layer(xp, mod_ctx, p, lam_init)
        for acc, t in zip(new_lists, ctx_new):
            acc.append(t.astype(x_prompt.dtype))
        ctx_cached = (cache_diff_k[:, l], cache_diff_v[:, l], cache_na_k[:, l], cache_na_v[:, l],
                      state_rwkv_fwd[:, l], state_rwkv_bwd[:, l])
        xs, _ = _layer(xs, mod_lat, p, lam_init, ctx_cached)
    y_prompt = _rmsnorm(xp, final_g)
    y_sample = _rmsnorm(xs, final_g)
    new_diff_k = jnp.stack(new_lists[0], axis=1)
    new_diff_v = jnp.stack(new_lists[1], axis=1)
    new_na_k = jnp.stack(new_lists[2], axis=1)
    new_na_v = jnp.stack(new_lists[3], axis=1)
    new_rwkv_fwd = jnp.stack(new_lists[4], axis=1)
    new_rwkv_bwd = jnp.stack(new_lists[5], axis=1)
    return (y_prompt, y_sample, new_diff_k, new_diff_v, new_na_k, new_na_v, new_rwkv_fwd, new_rwkv_bwd)
```

```python
import functools
import math

import jax
import jax.numpy as jnp
import numpy as np
from jax import lax
from jax.experimental import pallas as pl
from jax.experimental.pallas import tpu as pltpu

F32 = jnp.float32
BF16 = jnp.bfloat16

GRID_W = 64
HD_A = 64
DK_B = 64
HD_C = 64
WIN_R = 8
WIN_C = 16
GW = 1024
R_SMALL = 512
N_MOD = 6
GN_EPS = 64e-5
EPS = 1e-6
ROPE_BASE = 10000.0
HY_BANDS = 16
HY_TARGET = 1e-2
HY_FAST = 0.3
HY_SLOW = 1.5
NEG = -1e30
LANES = 128
VMEM_LIMIT = 56 * 1024 * 1024


def _cp(*sem):
    return pltpu.CompilerParams(dimension_semantics=sem, vmem_limit_bytes=VMEM_LIMIT)


def _split_bf16(x):
    hi = x.astype(BF16)
    lo = (x - hi.astype(F32)).astype(BF16)
    return hi, lo


def _dot(a, b):
    return jnp.dot(a, b, preferred_element_type=F32)


def _dot_nt(a, b):
    return lax.dot_general(a, b, (((1,), (1,)), ((), ())), preferred_element_type=F32)


def _dot3(a, b):
    ah, al = _split_bf16(a)
    bh, bl = _split_bf16(b)
    return _dot(ah, bh) + _dot(al, bh) + _dot(ah, bl)


def _ones_bd(n=LANES, blk=HD_A):
    r = lax.broadcasted_iota(jnp.int32, (n, n), 0) // blk
    c = lax.broadcasted_iota(jnp.int32, (n, n), 1) // blk
    return jnp.where(r == c, 1.0, 0.0).astype(BF16)


def _headsum(x, ones_bd):
    hi, lo = _split_bf16(x)
    outs = []
    for g in range(x.shape[1] // LANES):
        sl = slice(g * LANES, (g + 1) * LANES)
        outs.append(_dot(hi[:, sl], ones_bd) + _dot(lo[:, sl], ones_bd))
    return jnp.concatenate(outs, axis=1) if len(outs) > 1 else outs[0]


def _group_fn(mp, ts):
    def g(row0):
        return jnp.where(row0 < mp, 0, 1 + (row0 - mp) // ts)
    return g


def _mod_kernel(c_ref, w_ref, b_ref, o_ref):
    c = c_ref[...]
    s = c * jax.nn.sigmoid(c)
    o_ref[...] = _dot3(s, w_ref[...]) + b_ref[...]


def _modulation(cv, w_mod, b_mod):
    nl, d, n = w_mod.shape
    tn = 512
    return pl.pallas_call(
        _mod_kernel,
        grid=(nl, n // tn),
        in_specs=[pl.BlockSpec((8, d), lambda l, j: (0, 0)),
                  pl.BlockSpec((None, d, tn), lambda l, j: (l, 0, j)),
                  pl.BlockSpec((None, 1, tn), lambda l, j: (l, 0, j))],
        out_specs=pl.BlockSpec((None, 8, tn), lambda l, j: (l, 0, j)),
        out_shape=jax.ShapeDtypeStruct((nl, 8, n), F32),
        compiler_params=_cp("arbitrary", "arbitrary"),
        name="modulation",
    )(cv, w_mod, b_mod.reshape(nl, 1, n))


def _norm_mod_kernel(x_ref, g_ref, sc_ref, sh_ref, o_ref):
    x = x_ref[...]
    ms = jnp.mean(x * x, axis=-1, keepdims=True)
    y = x * lax.rsqrt(ms + EPS) * g_ref[...]
    o_ref[...] = (y * (1.0 + sc_ref[...]) + sh_ref[...]).astype(o_ref.dtype)


def _norm_mod(x, g_all, mod, l, which_sc, which_sh, grp, tr=256):
    m, d = x.shape
    tr = min(tr, m)
    mspec = lambda w: pl.BlockSpec((None, None, None, 1, d), lambda i: (l, grp(i * tr), w, 0, 0))
    return pl.pallas_call(
        _norm_mod_kernel,
        grid=(m // tr,),
        in_specs=[pl.BlockSpec((tr, d), lambda i: (i, 0)),
                  pl.BlockSpec((None, 1, d), lambda i: (l, 0, 0)),
                  mspec(which_sc), mspec(which_sh)],
        out_specs=pl.BlockSpec((tr, d), lambda i: (i, 0)),
        out_shape=jax.ShapeDtypeStruct((m, d), BF16),
        compiler_params=_cp("arbitrary"),
        name="norm_mod",
    )(x, g_all, mod, mod)


def _rmsnorm_kernel(x_ref, g_ref, o_ref):
    x = x_ref[...]
    ms = jnp.mean(x * x, axis=-1, keepdims=True)
    o_ref[...] = x * lax.rsqrt(ms + EPS) * g_ref[...]


def _rmsnorm(x, g, tr=256):
    m, d = x.shape
    tr = min(tr, m)
    return pl.pallas_call(
        _rmsnorm_kernel,
        grid=(m // tr,),
        in_specs=[pl.BlockSpec((tr, d), lambda i: (i, 0)),
                  pl.BlockSpec((1, d), lambda i: (0, 0))],
        out_specs=pl.BlockSpec((tr, d), lambda i: (i, 0)),
        out_shape=jax.ShapeDtypeStruct((m, d), F32),
        compiler_params=_cp("arbitrary"),
        name="final_norm",
    )(x, g.reshape(1, d))


def _mm_fullk_kernel(*refs, n_a, epilogue):
    a_refs = refs[:n_a]
    w_ref = refs[n_a]
    o_ref = refs[-1]
    acc = None
    k0 = 0
    for a_ref in a_refs:
        kw = a_ref.shape[1]
        part = _dot(a_ref[...], w_ref[k0:k0 + kw, :])
        acc = part if acc is None else acc + part
        k0 += kw
    if epilogue == "relu2":
        acc = jnp.square(jnp.maximum(acc, 0.0))
    elif epilogue == "resid":
        x_ref, gate_ref = refs[n_a + 1], refs[n_a + 2]
        acc = x_ref[...] + gate_ref[...] * acc
    o_ref[...] = acc.astype(o_ref.dtype)


def _mm_fullk(a_list, w_all, l, out_dtype, epilogue="none", resid=None, tm=1024, tn=512, name="mm"):
    m = a_list[0].shape[0]
    k, n = w_all.shape[1], w_all.shape[2]
    tm, tn = min(tm, m), min(tn, n)
    in_specs = [pl.BlockSpec((tm, a.shape[1]), lambda i, j: (i, 0)) for a in a_list]
    in_specs.append(pl.BlockSpec((None, k, tn), lambda i, j: (l, 0, j)))
    args = list(a_list) + [w_all]
    if epilogue == "resid":
        x, mod, which, grp = resid
        d = mod.shape[-1]
        in_specs.append(pl.BlockSpec((tm, tn), lambda i, j: (i, j)))
        in_specs.append(pl.BlockSpec((None, None, None, 1, tn),
                                     lambda i, j: (l, grp(i * tm), which, 0, j)))
        args += [x, mod]
    return pl.pallas_call(
        functools.partial(_mm_fullk_kernel, n_a=len(a_list), epilogue=epilogue),
        grid=(m // tm, n // tn),
        in_specs=in_specs,
        out_specs=pl.BlockSpec((tm, tn), lambda i, j: (i, j)),
        out_shape=jax.ShapeDtypeStruct((m, n), out_dtype),
        compiler_params=_cp("arbitrary", "arbitrary"),
        name=name,
    )(*args)


def _mm_kloop_kernel(a_ref, w_ref, x_ref, gate_ref, o_ref, acc_ref):
    kk = pl.program_id(2)

    @pl.when(kk == 0)
    def _():
        acc_ref[...] = jnp.zeros_like(acc_ref)

    acc_ref[...] += _dot(a_ref[...], w_ref[...])

    @pl.when(kk == pl.num_programs(2) - 1)
    def _():
        o_ref[...] = x_ref[...] + gate_ref[...] * acc_ref[...]


def _mm_kloop_resid(a, w_all, l, x, mod, which, grp, tm=1024, tn=1024, tk=1024):
    m, k = a.shape
    n = w_all.shape[2]
    tm, tn, tk = min(tm, m), min(tn, n), min(tk, k)
    return pl.pallas_call(
        _mm_kloop_kernel,
        grid=(m // tm, n // tn, k // tk),
        in_specs=[pl.BlockSpec((tm, tk), lambda i, j, q: (i, q)),
                  pl.BlockSpec((None, tk, tn), lambda i, j, q: (l, q, j)),
                  pl.BlockSpec((tm, tn), lambda i, j, q: (i, j)),
                  pl.BlockSpec((None, None, None, 1, tn),
                               lambda i, j, q: (l, grp(i * tm), which, 0, j))],
        out_specs=pl.BlockSpec((tm, tn), lambda i, j, q: (i, j)),
        out_shape=jax.ShapeDtypeStruct((m, n), F32),
        scratch_shapes=[pltpu.VMEM((tm, tn), F32)],
        compiler_params=_cp("arbitrary", "arbitrary", "arbitrary"),
        name="ffn_down",
    )(a, w_all, x, mod)


def _softplus(z):
    return jnp.maximum(z, 0.0) + jnp.log(1.0 + jnp.exp(-jnp.abs(z)))


def _rwkv_prep_kernel(k_ref, sm_ref, w0_ref, w2_ref, a0_ref, a2_ref, g2_ref, kk_ref_p, ka_ref_p,
                      kk_o, wf_o, wb_o, bf_o, bb_o, kdf_o, kdb_o, g_o):
    ones_bd = _ones_bd()
    k = k_ref[...]
    kraw = k * kk_ref_p[...]
    ss = _headsum(kraw * kraw, ones_bd)
    kk = kraw / jnp.maximum(jnp.sqrt(ss), 1e-12)
    kk_o[...] = kk
    sm = sm_ref[...]
    wd = jnp.tanh(sm[:, 0:LANES])
    ad = sm[:, LANES:2 * LANES]
    gd = jax.nn.sigmoid(sm[:, 2 * LANES:4 * LANES])
    ka = ka_ref_p[...]
    for d, (w_o, b_o, kd_o) in enumerate(((wf_o, bf_o, kdf_o), (wb_o, bb_o, kdb_o))):
        wl = w0_ref[d:d + 1, :] + _dot3(wd, w2_ref[d])
        wlog = -_softplus(-wl) - 0.5
        w_o[...] = jnp.exp(-jnp.exp(wlog))
        a = jax.nn.sigmoid(a0_ref[d:d + 1, :] + _dot3(ad, a2_ref[d]))
        b_o[...] = kk * a
        kd_o[...] = k * (1.0 + (a - 1.0) * ka)
    g_o[...] = _dot3(gd, g2_ref[...])


def _rwkv_prep(y, row0, m, prm, l, tr=256):
    tr = min(tr, m)
    r0 = row0 // tr
    nsm = y.shape[1] // R_SMALL - 1
    out = jax.ShapeDtypeStruct((m, GW), F32)
    ospec = pl.BlockSpec((tr, GW), lambda i: (i, 0))
    pspec = lambda shape: pl.BlockSpec((None,) + shape, lambda i: (l,) + (0,) * len(shape))
    return pl.pallas_call(
        _rwkv_prep_kernel,
        grid=(m // tr,),
        in_specs=[pl.BlockSpec((tr, GW), lambda i: (r0 + i, 1)),
                  pl.BlockSpec((tr, R_SMALL), lambda i: (r0 + i, nsm)),
                  pspec((2, GW)), pspec((2, LANES, GW)), pspec((2, GW)), pspec((2, LANES, GW)),
                  pspec((2 * LANES, GW)), pspec((1, GW)), pspec((1, GW))],
        out_specs=[ospec] * 8,
        out_shape=[out] * 8,
        compiler_params=_cp("arbitrary"),
        name="rwkv_prep",
    )(y, y, prm["w0"], prm["w2p"], prm["a0"], prm["a2p"], prm["g2p"], prm["k_k"], prm["k_a"])


def _rwkv_scan_kernel(kk_f, r_f, v_f, w_f, b_f, kd_f, kk_b, r_b, v_b, w_b, b_b, kd_b,
                      s0f_ref, s0b_ref, of_ref, ob_ref, sf_ref, sb_ref, st_ref, *, tc):
    c = pl.program_id(1)

    @pl.when(c == 0)
    def _():
        st_ref[0] = s0f_ref[...]
        st_ref[1] = s0b_ref[...]

    ones_bd = _ones_bd()
    rowi = lax.broadcasted_iota(jnp.int32, (HD_A, LANES), 0)
    lane = lax.broadcasted_iota(jnp.int32, (HD_A, LANES), 1) % HD_A
    eye2 = jnp.where(rowi == lane, 1.0, 0.0)
    of_ref[...] = jnp.zeros_like(of_ref)
    ob_ref[...] = jnp.zeros_like(ob_ref)
    ngrp = st_ref.shape[2] // LANES
    dirs = ((kk_f, r_f, v_f, w_f, b_f, kd_f, of_ref), (kk_b, r_b, v_b, w_b, b_b, kd_b, ob_ref))

    sub = 8

    def step(i, carry):
        for j in range(sub):
            for d, (kk_r, r_r, v_r, w_r, b_r, kd_r, o_r) in enumerate(dirs):
                base = pl.multiple_of(i * sub if d == 0 else tc - sub - i * sub, sub)
                rj = j if d == 0 else sub - 1 - j
                sel = lane == base + rj
                for g in range(ngrp):
                    sl = pl.ds(g * LANES, LANES)
                    row = lambda ref: jnp.broadcast_to(ref[pl.ds(base, sub), sl][rj:rj + 1, :], (HD_A, LANES))
                    s = st_ref[d, :, sl]
                    x = (s * row(kk_r)).astype(BF16)
                    ev = (eye2 * row(v_r)).astype(BF16)
                    red = _dot(jnp.concatenate([x, ev], axis=0), ones_bd)
                    s = s * row(w_r) - red[:HD_A] * row(b_r) + red[HD_A:] * row(kd_r)
                    st_ref[d, :, sl] = s
                    ro = _dot((s * row(r_r)).astype(BF16), ones_bd)
                    o_r[g] = jnp.where(sel, ro, o_r[g])
        return carry

    lax.fori_loop(0, tc // sub, step, 0)

    @pl.when(c == pl.num_programs(1) - 1)
    def _():
        sf_ref[...] = st_ref[0]
        sb_ref[...] = st_ref[1]


def _rwkv_scan(y, row0, nb, t, pre, s0f, s0b, tc=64):
    kk, wf, wb, bf, bb, kdf, kdb = pre
    assert tc == HD_A
    nc = t // tc
    yb = row0 // tc
    ngrp = GW // LANES
    fw = lambda col: pl.BlockSpec((tc, GW), lambda b, c: (yb + b * nc + c, col))
    bw = lambda col: pl.BlockSpec((tc, GW), lambda b, c: (yb + b * nc + nc - 1 - c, col))
    fwp = pl.BlockSpec((tc, GW), lambda b, c: (b * nc + c, 0))
    bwp = pl.BlockSpec((tc, GW), lambda b, c: (b * nc + nc - 1 - c, 0))
    sspec = pl.BlockSpec((None, HD_A, GW), lambda b, c: (b, 0, 0))
    oshape = jax.ShapeDtypeStruct((nb, nc, ngrp, HD_A, LANES), F32)
    sshape = jax.ShapeDtypeStruct((nb, HD_A, GW), F32)
    return pl.pallas_call(
        functools.partial(_rwkv_scan_kernel, tc=tc),
        grid=(nb, nc),
        in_specs=[fwp, fw(0), fw(2), fwp, fwp, fwp, bwp, bw(0), bw(2), bwp, bwp, bwp, sspec, sspec],
        out_specs=[pl.BlockSpec((None, None, ngrp, HD_A, LANES), lambda b, c: (b, c, 0, 0, 0)),
                   pl.BlockSpec((None, None, ngrp, HD_A, LANES), lambda b, c: (b, nc - 1 - c, 0, 0, 0)),
                   sspec, sspec],
        out_shape=[oshape, oshape, sshape, sshape],
        scratch_shapes=[pltpu.VMEM((2, HD_A, GW), F32)],
        compiler_params=_cp("arbitrary", "arbitrary"),
        name="rwkv_scan",
    )(kk, y, y, wf, bf, kdf, kk, y, y, wb, bb, kdb, s0f, s0b)


def _rwkv_post_kernel(of_ref, ob_ref, r_ref, v_ref, kdf_ref, kdb_ref, g_ref, lnw_ref, lnb_ref, rk_ref, o_ref):
    ones_bd = _ones_bd()
    o = of_ref[...] + ob_ref[...]
    mu = _headsum(o, ones_bd) * (1.0 / HD_A)
    dlt = o - mu
    var = _headsum(dlt * dlt, ones_bd) * (1.0 / HD_A)
    on = dlt * lax.rsqrt(var + GN_EPS) * lnw_ref[...] + lnb_ref[...]
    bonus = _headsum(r_ref[...] * (kdf_ref[...] + kdb_ref[...]) * rk_ref[...], ones_bd) * v_ref[...]
    o_ref[...] = ((on + bonus) * g_ref[...]).astype(o_ref.dtype)


def _rwkv_post(of, ob, y, row0, m, kdf, kdb, g, prm, l, tr=256):
    tr = min(tr, m)
    r0 = row0 // tr
    rs = pl.BlockSpec((tr, GW), lambda i: (i, 0))
    ps = pl.BlockSpec((None, 1, GW), lambda i: (l, 0, 0))
    return pl.pallas_call(
        _rwkv_post_kernel,
        grid=(m // tr,),
        in_specs=[rs, rs, pl.BlockSpec((tr, GW), lambda i: (r0 + i, 0)),
                  pl.BlockSpec((tr, GW), lambda i: (r0 + i, 2)), rs, rs, rs, ps, ps, ps],
        out_specs=rs,
        out_shape=jax.ShapeDtypeStruct((m, GW), BF16),
        compiler_params=_cp("arbitrary"),
        name="rwkv_post",
    )(of, ob, y, y, kdf, kdb, g, prm["ln_w"], prm["ln_b"], prm["r_k"])


def _rwkv_mixer(y, row0, nb, t, prm, l, s0f, s0b):
    m = nb * t
    kk, wf, wb, bf, bb, kdf, kdb, g = _rwkv_prep(y, row0, m, prm, l)
    of_t, ob_t, sf, sb = _rwkv_scan(y, row0, nb, t, (kk, wf, wb, bf, bb, kdf, kdb), s0f, s0b)
    nc = of_t.shape[1]
    untr = lambda o: o.reshape(nb, nc, GW // LANES, HD_A, 2, HD_A).transpose(0, 1, 5, 2, 4, 3).reshape(m, GW)
    ya = _rwkv_post(untr(of_t), untr(ob_t), y, row0, m, kdf, kdb, g, prm, l)
    return ya, sf, sb


def _state_in(s):
    b, h, v, k = s.shape
    return s.transpose(0, 2, 1, 3).reshape(b, v, h * k)


def _state_out(s):
    b, v, hk = s.shape
    return s.reshape(b, v, hk // HD_A, HD_A).transpose(0, 2, 1, 3)


def _rope_tables(t):
    pos = jnp.arange(t)
    row = (pos // GRID_W).astype(F32)
    col = (pos % GRID_W).astype(F32)
    lane = jnp.arange(LANES)
    dd = lane % DK_B
    half = DK_B // 2
    nf = half // 2
    part = dd // half
    ii = dd % half
    inv = ROPE_BASE ** (-jnp.arange(nf, dtype=F32) / nf)
    p = jnp.where(part[None, :] == 0, row[:, None], col[:, None])
    ang = p * inv[ii % nf][None, :]
    cos, sin = jnp.cos(ang), jnp.sin(ang)
    first = (ii < nf)[None, :]
    return cos, jnp.where(first, -sin, 0.0), jnp.where(first, 0.0, sin), nf


def _rope(x, cos, s1, s2, nf):
    return x * cos + pltpu.roll(x, LANES - nf, 1) * s1 + pltpu.roll(x, nf, 1) * s2


def _lam(lam_ref, lam_init):
    lv = lam_ref[...]
    a = jnp.sum(lv[0:1] * lv[1:2], axis=-1, keepdims=True)
    b = jnp.sum(lv[2:3] * lv[3:4], axis=-1, keepdims=True)
    return jnp.exp(a) - jnp.exp(b) + lam_init


def _diff_ctx_kernel(q_ref, k_ref, v_ref, lam_ref, g_ref, o_ref, ko_ref, vo_ref, *, lam_init):
    lane = lax.broadcasted_iota(jnp.int32, q_ref.shape, 1)
    q = q_ref[...] * (DK_B ** -0.5)
    k = k_ref[...]
    v = v_ref[...]
    ko_ref[...] = k
    vo_ref[...] = v
    kb = k.astype(BF16)
    lam = _lam(lam_ref, lam_init)
    ps = []
    for h in range(2):
        qh = jnp.where((lane // DK_B) == h, q, 0.0).astype(BF16)
        s = _dot_nt(qh, kb)
        e = jnp.exp(s - jnp.max(s, axis=-1, keepdims=True))
        ps.append(e / jnp.sum(e, axis=-1, keepdims=True))
    p = (ps[0] - lam * ps[1]).astype(BF16)
    o = _dot(p, v.astype(BF16))
    ms = jnp.mean(o * o, axis=-1, keepdims=True)
    o_ref[...] = (o * lax.rsqrt(ms + EPS) * g_ref[...] * (1.0 - lam_init)).astype(o_ref.dtype)


def _diff_ctx(y, row0, nb, t, lam_all, g_all, l, lam_init, col0):
    nh = GW // LANES
    rb = row0 // t
    cq, ck, cv = col0 // LANES, (col0 + GW) // LANES, (col0 + 2 * GW) // LANES
    ysp = lambda cb: pl.BlockSpec((t, LANES), lambda b, h: (rb + b, cb + h))
    cache = jax.ShapeDtypeStruct((nb, nh, t, LANES), F32)
    cspec = pl.BlockSpec((None, None, t, LANES), lambda b, h: (b, h, 0, 0))
    return pl.pallas_call(
        functools.partial(_diff_ctx_kernel, lam_init=lam_init),
        grid=(nb, nh),
        in_specs=[ysp(cq), ysp(ck), ysp(cv),
                  pl.BlockSpec((None, 4, DK_B), lambda b, h: (l, 0, 0)),
                  pl.BlockSpec((None, 1, LANES), lambda b, h: (l, 0, 0))],
        out_specs=[pl.BlockSpec((t, LANES), lambda b, h: (b, h)), cspec, cspec],
        out_shape=[jax.ShapeDtypeStruct((nb * t, GW), BF16), cache, cache],
        compiler_params=_cp("arbitrary", "arbitrary"),
        name="diff_ctx",
    )(y, y, y, lam_all, g_all)


def _diff_lat_kernel(q_ref, k_ref, v_ref, ck_ref, cv_ref, cos_ref, s1_ref, s2_ref, lam_ref, g_ref,
                     o_ref, kb_ref, vb_ref, *, lam_init, tq, nf):
    t = q_ref.shape[0]
    cos, s1, s2 = cos_ref[...], s1_ref[...], s2_ref[...]
    kb_ref[...] = _rope(k_ref[...], cos, s1, s2, nf).astype(BF16)
    vb_ref[...] = v_ref[...].astype(BF16)
    ckb = ck_ref[...].astype(BF16)
    cvb = cv_ref[...].astype(BF16)
    lam = _lam(lam_ref, lam_init)
    lane = lax.broadcasted_iota(jnp.int32, (tq, LANES), 1)
    gsc = g_ref[...] * (1.0 - lam_init)

    def blk(i, carry):
        rows = pl.ds(pl.multiple_of(i * tq, tq), tq)
        q = _rope(q_ref[rows, :] * (DK_B ** -0.5), cos_ref[rows, :], s1_ref[rows, :], s2_ref[rows, :], nf)
        pcs, pls = [], []
        for h in range(2):
            qh = jnp.where((lane // DK_B) == h, q, 0.0).astype(BF16)
            sc = _dot_nt(qh, ckb)
            sl = _dot_nt(qh, kb_ref[...])
            mx = jnp.maximum(jnp.max(sc, axis=-1, keepdims=True), jnp.max(sl, axis=-1, keepdims=True))
            ec = jnp.exp(sc - mx)
            el = jnp.exp(sl - mx)
            z = jnp.sum(ec, axis=-1, keepdims=True) + jnp.sum(el, axis=-1, keepdims=True)
            pcs.append(ec / z)
            pls.append(el / z)
        pc = (pcs[0] - lam * pcs[1]).astype(BF16)
        plat = (pls[0] - lam * pls[1]).astype(BF16)
        o = _dot(pc, cvb) + _dot(plat, vb_ref[...])
        ms = jnp.mean(o * o, axis=-1, keepdims=True)
        o_ref[rows, :] = (o * lax.rsqrt(ms + EPS) * gsc).astype(o_ref.dtype)
        return carry

    lax.fori_loop(0, t // tq, blk, 0)


def _diff_lat(y, row0, nb, t, cache_k, cache_v, lam_all, g_all, l, lam_init, col0, tq=256):
    nh = GW // LANES
    rb = row0 // t
    tq = min(tq, t)
    cq, ck, cv = col0 // LANES, (col0 + GW) // LANES, (col0 + 2 * GW) // LANES
    ysp = lambda cb: pl.BlockSpec((t, LANES), lambda b, h: (rb + b, cb + h))
    past = cache_k.shape[3]
    csp = pl.BlockSpec((None, None, None, past, LANES), lambda b, h: (b, l, h, 0, 0))
    cos, s1, s2, nf = _rope_tables(t)
    tsp = pl.BlockSpec((t, LANES), lambda b, h: (0, 0))
    return pl.pallas_call(
        functools.partial(_diff_lat_kernel, lam_init=lam_init, tq=tq, nf=nf),
        grid=(nb, nh),
        in_specs=[ysp(cq), ysp(ck), ysp(cv), csp, csp, tsp, tsp, tsp,
                  pl.BlockSpec((None, 4, DK_B), lambda b, h: (l, 0, 0)),
                  pl.BlockSpec((None, 1, LANES), lambda b, h: (l, 0, 0))],
        out_specs=pl.BlockSpec((t, LANES), lambda b, h: (b, h)),
        out_shape=jax.ShapeDtypeStruct((nb * t, GW), BF16),
        scratch_shapes=[pltpu.VMEM((t, LANES), BF16), pltpu.VMEM((t, LANES), BF16)],
        compiler_params=_cp("arbitrary", "arbitrary"),
        name="diff_lat",
    )(y, y, y, cache_k, cache_v, cos, s1, s2, lam_all, g_all)


def _na_ctx_kernel(q_ref, k_ref, v_ref, o_ref):
    lane = lax.broadcasted_iota(jnp.int32, q_ref.shape, 1)
    q = q_ref[...] * (HD_C ** -0.5)
    kb = k_ref[...].astype(BF16)
    vb = v_ref[...].astype(BF16)
    outs = []
    for h in range(2):
        qh = jnp.where((lane // HD_C) == h, q, 0.0).astype(BF16)
        s = _dot_nt(qh, kb)
        e = jnp.exp(s - jnp.max(s, axis=-1, keepdims=True))
        p = e / jnp.sum(e, axis=-1, keepdims=True)
        outs.append(_dot(p.astype(BF16), vb))
    o_ref[...] = jnp.where((lane // HD_C) == 0, outs[0], outs[1]).astype(o_ref.dtype)


def _na_ctx(y, row0, nb, t, col0):
    npair = GW // LANES
    rb = row0 // t
    cq, ck, cv = col0 // LANES, (col0 + GW) // LANES, (col0 + 2 * GW) // LANES
    ysp = lambda cb: pl.BlockSpec((t, LANES), lambda b, p: (rb + b, cb + p))
    return pl.pallas_call(
        _na_ctx_kernel,
        grid=(nb, npair),
        in_specs=[ysp(cq), ysp(ck), ysp(cv)],
        out_specs=pl.BlockSpec((t, LANES), lambda b, p: (b, p)),
        out_shape=jax.ShapeDtypeStruct((nb * t, GW), BF16),
        compiler_params=_cp("arbitrary", "arbitrary"),
        name="na_ctx",
    )(y, y, y)


def _na_bias_table(rpb):
    i = jnp.arange(WIN_R)
    w = jnp.arange(GRID_W)
    c0 = jnp.clip(w - WIN_C // 2, 0, GRID_W - WIN_C)
    allowed = (w[None, :] >= c0[:, None]) & (w[None, :] < c0[:, None] + WIN_C)
    dc = jnp.clip(w[None, :] - w[:, None] + (WIN_C - 1), 0, 2 * WIN_C - 2)
    dr = i[None, :] - i[:, None] + (WIN_R - 1)
    tbl = rpb[:, dr[:, :, None, None], dc[None, None, :, :]]
    tbl = jnp.where(allowed[None, None, None], tbl, NEG)
    h = rpb.shape[0]
    return tbl.transpose(0, 1, 3, 2, 4).reshape(h, WIN_R, GRID_W, WIN_R * GRID_W)


def _na_lat_kernel(q_ref, k_ref, v_ref, ck_ref, cv_ref, bias_ref, o_ref, kb_ref, vb_ref):
    t = q_ref.shape[0]
    rows = t // GRID_W
    kr = min(WIN_R, rows)
    kb_ref[...] = k_ref[...].astype(BF16)
    vb_ref[...] = v_ref[...].astype(BF16)
    ckb = ck_ref[...].astype(BF16)
    cvb = cv_ref[...].astype(BF16)
    lane = lax.broadcasted_iota(jnp.int32, (GRID_W, LANES), 1)

    def row_step(r, carry):
        r0 = jnp.clip(r - kr // 2, 0, rows - kr)
        pat = r - r0
        qrows = pl.ds(pl.multiple_of(r * GRID_W, GRID_W), GRID_W)
        wrows = pl.ds(pl.multiple_of(r0 * GRID_W, GRID_W), kr * GRID_W)
        q = q_ref[qrows, :] * (HD_C ** -0.5)
        kw = kb_ref[wrows, :]
        vw = vb_ref[wrows, :]
        outs = []
        for h in range(2):
            qh = jnp.where((lane // HD_C) == h, q, 0.0).astype(BF16)
            sw = _dot_nt(qh, kw) + bias_ref[h, pat]
            sc = _dot_nt(qh, ckb)
            mx = jnp.maximum(jnp.max(sw, axis=-1, keepdims=True), jnp.max(sc, axis=-1, keepdims=True))
            ew = jnp.exp(sw - mx)
            ec = jnp.exp(sc - mx)
            z = jnp.sum(ew, axis=-1, keepdims=True) + jnp.sum(ec, axis=-1, keepdims=True)
            outs.append(_dot((ew / z).astype(BF16), vw) + _dot((ec / z).astype(BF16), cvb))
        o_ref[qrows, :] = jnp.where((lane // HD_C) == 0, outs[0], outs[1]).astype(o_ref.dtype)
        return carry

    lax.fori_loop(0, rows, row_step, 0)


def _na_lat(y, row0, nb, t, ctx_k, ctx_v, bias, l, col0):
    npair = GW // LANES
    rb = row0 // t
    cq, ck, cv = col0 // LANES, (col0 + GW) // LANES, (col0 + 2 * GW) // LANES
    ysp = lambda cb: pl.BlockSpec((t, LANES), lambda b, p: (rb + b, cb + p))
    past = ctx_k.shape[3]
    csp = pl.BlockSpec((None, None, None, past, LANES), lambda b, p: (b, l, p, 0, 0))
    bsp = pl.BlockSpec((None, 2, WIN_R, GRID_W, WIN_R * GRID_W), lambda b, p: (l, p, 0, 0, 0))
    return pl.pallas_call(
        _na_lat_kernel,
        grid=(nb, npair),
        in_specs=[ysp(cq), ysp(ck), ysp(cv), csp, csp, bsp],
        out_specs=pl.BlockSpec((t, LANES), lambda b, p: (b, p)),
        out_shape=jax.ShapeDtypeStruct((nb * t, GW), BF16),
        scratch_shapes=[pltpu.VMEM((t, LANES), BF16), pltpu.VMEM((t, LANES), BF16)],
        compiler_params=_cp("arbitrary", "arbitrary"),
        name="na_lat",
    )(y, y, y, ctx_k, ctx_v, bias)


def _hy_positions(length):
    t = jnp.linspace(0.0, 1.0, length, dtype=F32)[:, None]
    ang = 2.0 * math.pi * jnp.arange(length, dtype=F32) / length
    bands = jnp.linspace(1e-4, HY_BANDS - 1, HY_BANDS, dtype=F32)
    z = jnp.concatenate([t, jnp.cos(ang[:, None] * bands[None, :]),
                         -jnp.sin(ang[:, None] * bands[None, :])], axis=-1)
    z = jnp.pad(z, ((0, 0), (0, LANES - z.shape[1])))
    mn = math.log(HY_TARGET) / HY_SLOW
    mx = math.log(HY_TARGET) / HY_FAST
    deltas = jnp.abs(jnp.linspace(mn, mx, GW, dtype=F32))
    decay = jnp.exp(-t * deltas[None, :])
    return z, decay


def _hy_filter_kernel(z_ref, dec_ref, w1_ref, b1_ref, f1_ref, w2_ref, b2_ref, f2_ref, w3_ref, o_ref):
    hid = jnp.sin(f1_ref[...] * (_dot3(z_ref[...], w1_ref[...]) + b1_ref[...]))
    hid = jnp.sin(f2_ref[...] * (_dot3(hid, w2_ref[...]) + b2_ref[...]))
    h = _dot3(hid, w3_ref[...])
    dec = dec_ref[...]
    first = (lax.broadcasted_iota(jnp.int32, dec.shape, 0) + pl.program_id(0) * dec.shape[0]) == 0
    for o in range(h.shape[1] // (2 * GW)):
        hf = h[:, (2 * o) * GW:(2 * o + 1) * GW] * dec
        hb = jnp.where(first, 0.0, h[:, (2 * o + 1) * GW:(2 * o + 2) * GW] * dec)
        o_ref[:, (2 * o) * GW:(2 * o + 1) * GW] = hf + hb
        o_ref[:, (2 * o + 1) * GW:(2 * o + 2) * GW] = hf - hb


def _hy_filters(length, prm, l):
    z, decay = _hy_positions(length)
    n3 = prm["w3"].shape[2]
    hid = prm["w2"].shape[1]
    tl = min(length, 256)
    ps = lambda shape: pl.BlockSpec((None,) + shape, lambda i: (l,) + (0,) * len(shape))
    return pl.pallas_call(
        _hy_filter_kernel,
        grid=(length // tl,),
        in_specs=[pl.BlockSpec((tl, LANES), lambda i: (i, 0)), pl.BlockSpec((tl, GW), lambda i: (i, 0)),
                  ps((LANES, hid)), ps((1, hid)), ps((1, hid)), ps((hid, hid)), ps((1, hid)), ps((1, hid)),
                  ps((hid, n3))],
        out_specs=pl.BlockSpec((tl, n3), lambda i: (i, 0)),
        out_shape=jax.ShapeDtypeStruct((length, n3), F32),
        compiler_params=_cp("arbitrary"),
        name="hy_filter",
    )(z, decay, prm["w1p"], prm["b1"], prm["f1"], prm["w2"], prm["b2"], prm["f2"], prm["w3"])


def _dft_mats(length):
    n2 = 2 * length
    k = jnp.arange(length, dtype=jnp.int32)
    kn = (k[:, None] * k[None, :]) % n2
    ang = kn.astype(F32) * (2.0 * math.pi / n2)
    c, s = jnp.cos(ang), jnp.sin(ang)
    alt = jnp.where(k % 2 == 0, 1.0, -1.0).astype(F32)
    f_im = jnp.where((k == 0)[:, None], alt[None, :], -s)
    fwd = jnp.concatenate([c, f_im], axis=0).astype(BF16)
    ck = jnp.where(k == 0, 1.0, 2.0)[None, :]
    g_re = c * ck / n2
    g_im = jnp.where((k == 0)[None, :], alt[:, None] / n2, -2.0 * s / n2)
    inv = jnp.concatenate([g_re, g_im], axis=1).astype(BF16)
    return fwd, inv


def _mm_plain_kernel(a_ref, w_ref, o_ref):
    o_ref[...] = _dot(a_ref[...], w_ref[...].astype(BF16))


def _hy_spectrum(fwd, hk, tm=512, tn=512):
    m, k = fwd.shape
    n = hk.shape[1]
    tm, tn = min(tm, m), min(tn, n)
    return pl.pallas_call(
        _mm_plain_kernel,
        grid=(m // tm, n // tn),
        in_specs=[pl.BlockSpec((tm, k), lambda i, j: (i, 0)), pl.BlockSpec((k, tn), lambda i, j: (0, j))],
        out_specs=pl.BlockSpec((tm, tn), lambda i, j: (i, j)),
        out_shape=jax.ShapeDtypeStruct((m, n), F32),
        compiler_params=_cp("arbitrary", "arbitrary"),
        name="hy_spectrum",
    )(fwd, hk)


def _short_conv_kernel(u_ref, w_ref, b_ref, o_ref):
    u = u_ref[...]
    t = u.shape[0]
    rowi = lax.broadcasted_iota(jnp.int32, u.shape, 0)
    prev = jnp.where(rowi == 0, 0.0, pltpu.roll(u, 1, 0))
    nxt = jnp.where(rowi == t - 1, 0.0, pltpu.roll(u, t - 1, 0))
    o_ref[...] = prev * w_ref[0:1, :] + u * w_ref[1:2, :] + nxt * w_ref[2:3, :] + b_ref[...]


def _short_conv(y, row0, nb, t, w_all, b_all, l, col0, width, tc=512):
    rb = row0 // t
    cb = col0 // tc
    return pl.pallas_call(
        _short_conv_kernel,
        grid=(nb, width // tc),
        in_specs=[pl.BlockSpec((t, tc), lambda b, j: (rb + b, cb + j)),
                  pl.BlockSpec((None, 3, tc), lambda b, j: (l, 0, j)),
                  pl.BlockSpec((None, 1, tc), lambda b, j: (l, 0, j))],
        out_specs=pl.BlockSpec((t, tc), lambda b, j: (b, j)),
        out_shape=jax.ShapeDtypeStruct((nb * t, width), F32),
        compiler_params=_cp("arbitrary", "arbitrary"),
        name="short_conv",
    )(y, w_all, b_all)


def _hy_fwd_kernel(fre_ref, fim_ref, z_ref, kre_ref, kny_ref, kim_ref, o_ref):
    zb = z_ref[...].astype(BF16)
    re = _dot(fre_ref[...], zb)
    im = _dot(fim_ref[...], zb)
    tm = re.shape[0]
    dc = (lax.broadcasted_iota(jnp.int32, re.shape, 0) + pl.program_id(1) * tm) == 0
    kre = kre_ref[...]
    kim = jnp.where(dc, kny_ref[...], kim_ref[...])
    o_ref[0] = jnp.where(dc, re * kre, re * kre - im * kim).astype(o_ref.dtype)
    o_ref[1] = jnp.where(dc, im * kim, re * kim + im * kre).astype(o_ref.dtype)


def _hy_fwd(fwd, z, zcol, nb, t, kspec, o, tm=512, tn=512):
    tm, tn = min(tm, t), min(tn, GW)
    ni = t // tm
    zc = zcol // tn
    cs, cd = (2 * o) * GW // tn, (2 * o + 1) * GW // tn
    return pl.pallas_call(
        _hy_fwd_kernel,
        grid=(nb, ni, GW // tn),
        in_specs=[pl.BlockSpec((tm, t), lambda b, i, j: (i, 0)),
                  pl.BlockSpec((tm, t), lambda b, i, j: (ni + i, 0)),
                  pl.BlockSpec((t, tn), lambda b, i, j: (b, zc + j)),
                  pl.BlockSpec((tm, tn), lambda b, i, j: (i, cs + j)),
                  pl.BlockSpec((tm, tn), lambda b, i, j: (ni + i, cs + j)),
                  pl.BlockSpec((tm, tn), lambda b, i, j: (ni + i, cd + j))],
        out_specs=pl.BlockSpec((None, 2, tm, tn), lambda b, i, j: (b, 0, i, j)),
        out_shape=jax.ShapeDtypeStruct((nb, 2, t, GW), BF16),
        compiler_params=_cp("arbitrary", "arbitrary", "arbitrary"),
        name="hy_fwd",
    )(fwd, fwd, z, kspec, kspec, kspec)


def _hy_inv_kernel(g_ref, s_ref, zin_ref, gate_ref, skip_ref, o_ref):
    yv = _dot(g_ref[...], s_ref[...])
    o_ref[...] = (gate_ref[...] * (yv + skip_ref[...] * zin_ref[...])).astype(o_ref.dtype)


def _hy_inv(inv, spec, z, zcol, gate, gcol, skip_all, l, o, nb, t, out_dtype, tm=512, tn=512):
    tm, tn = min(tm, t), min(tn, GW)
    ni = t // tm
    zc, gc = zcol // tn, gcol // tn
    return pl.pallas_call(
        _hy_inv_kernel,
        grid=(nb, ni, GW // tn),
        in_specs=[pl.BlockSpec((tm, 2 * t), lambda b, i, j: (i, 0)),
                  pl.BlockSpec((None, 2 * t, tn), lambda b, i, j: (b, 0, j)),
                  pl.BlockSpec((tm, tn), lambda b, i, j: (b * ni + i, zc + j)),
                  pl.BlockSpec((tm, tn), lambda b, i, j: (b * ni + i, gc + j)),
                  pl.BlockSpec((None, None, 1, tn), lambda b, i, j: (l, o, 0, j))],
        out_specs=pl.BlockSpec((tm, tn), lambda b, i, j: (b * ni + i, j)),
        out_shape=jax.ShapeDtypeStruct((nb * t, GW), out_dtype),
        compiler_params=_cp("arbitrary", "arbitrary", "arbitrary"),
        name="hy_inv",
    )(inv, spec.reshape(nb, 2 * t, GW), z, gate, skip_all)


def _hyena_mixer(y, row0, nb, t, prm, l, col0, consts):
    fwd, inv, kspec = consts
    u = _short_conv(y, row0, nb, t, prm["sconv_w"], prm["sconv_b"], l, col0, 3 * GW)
    z, zcol = u, 0
    for o in range(2):
        spec = _hy_fwd(fwd, z, zcol, nb, t, kspec, o)
        z = _hy_inv(inv, spec, z, zcol, u, (o + 1) * GW, prm["skip"], l, o, nb, t,
                    BF16 if o == 1 else F32)
    return z


def _hyena_consts(t, prm, l):
    fwd, inv = _dft_mats(t)
    return fwd, inv, _hy_spectrum(fwd, _hy_filters(t, prm, l))


def _permute_w_in(w_in):
    gw = GW
    big = w_in[..., :3 * gw]
    o = 3 * gw
    small = w_in[..., o:o + 416]
    rest = w_in[..., o + 416:]
    pad = jnp.zeros(w_in.shape[:-1] + (R_SMALL - 416,), w_in.dtype)
    return jnp.concatenate([big, rest, small, pad], axis=-1).astype(BF16)


def _pad_rows(w, rows, at=0):
    pad = [(0, 0)] * (w.ndim - 2) + [(at, rows - at - w.shape[-2]), (0, 0)]
    return jnp.pad(w, pad)


def kernel(x_prompt, x_sample, cache_diff_k, cache_diff_v, cache_na_k, cache_na_v, state_rwkv_fwd,
           state_rwkv_bwd, c, c_ctx, norm1_g, norm2_g, w_mod, b_mod, w_in, rwkv_w0, rwkv_w2, rwkv_a0,
           rwkv_a2, rwkv_g2, rwkv_k_k, rwkv_k_a, rwkv_r_k, rwkv_ln_w, rwkv_ln_b, diff_lam,
           diff_subln_g, na_rpb, hy_sconv_w, hy_sconv_b, hy_f_w1, hy_f_b1, hy_f_freq1, hy_f_w2,
           hy_f_b2, hy_f_freq2, hy_f_w3, hy_skip, w_out, w_up, w_down, final_g):
    bp, tp, d = x_prompt.shape
    bs, ts, _ = x_sample.shape
    nl = w_in.shape[0]
    mp, ms = bp * tp, bs * ts
    assert mp % 1024 == 0 and ts % 1024 == 0 and bs + 1 <= 8
    grp = _group_fn(mp, ts)

    x = jnp.concatenate([x_prompt.reshape(mp, d), x_sample.reshape(ms, d)], axis=0)
    cv = jnp.concatenate([c_ctx[None], c, jnp.zeros((8 - 1 - bs, d), F32)], axis=0)
    mod = _modulation(cv, w_mod, b_mod).reshape(nl, 8, N_MOD, 1, d)

    w_in_b = _permute_w_in(w_in)
    w_out_b = w_out.astype(BF16)
    w_up_b = w_up.astype(BF16)
    w_down_b = w_down.astype(BF16)
    row = lambda p: p.reshape(nl, 1, -1)
    rw = {
        "w0": rwkv_w0, "a0": rwkv_a0,
        "w2p": jnp.stack([_pad_rows(rwkv_w2[:, 0], LANES, 0), _pad_rows(rwkv_w2[:, 1], LANES, 64)], axis=1),
        "a2p": jnp.stack([_pad_rows(rwkv_a2[:, 0], LANES, 0), _pad_rows(rwkv_a2[:, 1], LANES, 64)], axis=1),
        "g2p": _pad_rows(rwkv_g2, 2 * LANES, 0),
        "k_k": row(rwkv_k_k), "k_a": row(rwkv_k_a), "r_k": row(rwkv_r_k),
        "ln_w": row(rwkv_ln_w), "ln_b": row(rwkv_ln_b),
    }
    hy = {
        "sconv_w": hy_sconv_w, "sconv_b": hy_sconv_b.reshape(nl, 1, -1),
        "w1p": _pad_rows(hy_f_w1, LANES, 0), "b1": row(hy_f_b1), "f1": row(hy_f_freq1),
        "w2": hy_f_w2, "b2": row(hy_f_b2), "f2": row(hy_f_freq2), "w3": hy_f_w3,
        "skip": hy_skip.reshape(nl, 2, 1, GW),
    }
    subln = diff_subln_g.reshape(nl, 1, -1)
    norm1 = norm1_g.reshape(nl, 1, d)
    norm2 = norm2_g.reshape(nl, 1, d)
    na_bias = jax.vmap(_na_bias_table)(na_rpb)
    pair = lambda a: a.reshape(a.shape[0], a.shape[1], a.shape[2] // 2, 2, a.shape[3], a.shape[4]) \
        .transpose(0, 1, 2, 4, 3, 5).reshape(a.shape[0], a.shape[1], a.shape[2] // 2, a.shape[3], 2 * a.shape[4])
    na_ck, na_cv = pair(cache_na_k), pair(cache_na_v)
    zero_state = jnp.zeros((bp, HD_A, GW), F32)

    c_b, c_c, c_d = 3 * GW, 6 * GW, 9 * GW
    new = [[] for _ in range(6)]
    for l in range(nl):
        lam_init = 0.8 - 0.6 * math.exp(-0.3 * l)
        h = _norm_mod(x, norm1, mod, l, 1, 0, grp)
        y = _mm_fullk([h], w_in_b, l, F32, name="proj_in")

        ya_p, sf, sb = _rwkv_mixer(y, 0, bp, tp, rw, l, zero_state, zero_state)
        yb_p, dk, dv = _diff_ctx(y, 0, bp, tp, diff_lam, subln, l, lam_init, c_b)
        yc_p = _na_ctx(y, 0, bp, tp, c_c)
        yd_p = _hyena_mixer(y, 0, bp, tp, hy, l, c_d, _hyena_consts(tp, hy, l))
        ya_s, _, _ = _rwkv_mixer(y, mp, bs, ts, rw, l, _state_in(state_rwkv_fwd[:, l]),
                                 _state_in(state_rwkv_bwd[:, l]))
        yb_s = _diff_lat(y, mp, bs, ts, cache_diff_k, cache_diff_v, diff_lam, subln, l, lam_init, c_b)
        yc_s = _na_lat(y, mp, bs, ts, na_ck, na_cv, na_bias, l, c_c)
        yd_s = _hyena_mixer(y, mp, bs, ts, hy, l, c_d, _hyena_consts(ts, hy, l))

        mix = [jnp.concatenate([p_, s_], axis=0) for p_, s_ in
               ((ya_p, ya_s), (yb_p, yb_s), (yc_p, yc_s), (yd_p, yd_s))]
        x = _mm_fullk(mix, w_out_b, l, F32, epilogue="resid", resid=(x, mod, 2, grp), name="proj_out")
        h = _norm_mod(x, norm2, mod, l, 4, 3, grp)
        u = _mm_fullk([h], w_up_b, l, BF16, epilogue="relu2", name="ffn_up")
        x = _mm_kloop_resid(u, w_down_b, l, x, mod, 5, grp)

        yp = y[:mp]
        hc = lambda cols, nh, hd: cols.reshape(bp, tp, nh, hd).transpose(0, 2, 1, 3)
        new[0].append(dk)
        new[1].append(dv)
        new[2].append(hc(yp[:, c_c + GW:c_c + 2 * GW], GW // HD_C, HD_C))
        new[3].append(hc(yp[:, c_c + 2 * GW:c_c + 3 * GW], GW // HD_C, HD_C))
        new[4].append(_state_out(sf))
        new[5].append(_state_out(sb))

    yout = _rmsnorm(x, final_g)
    outs = [yout[:mp].reshape(bp, tp, d), yout[mp:].reshape(bs, ts, d)]
    outs += [jnp.stack(n, axis=1) for n in new]
    return tuple(outs)
```

```python
import functools
import math

import jax
import jax.numpy as jnp
import numpy as np
from jax import lax
from jax.experimental import pallas as pl
from jax.experimental.pallas import tpu as pltpu

F32 = jnp.float32
BF16 = jnp.bfloat16

GRID_W = 64
HD_A = 64
DK_B = 64
HD_C = 64
WIN_R = 8
WIN_C = 16
GW = 1024
R_SMALL = 512
N_MOD = 6
GN_EPS = 64e-5
EPS = 1e-6
ROPE_BASE = 10000.0
HY_BANDS = 16
HY_TARGET = 1e-2
HY_FAST = 0.3
HY_SLOW = 1.5
NEG = -1e30
LANES = 128
VMEM_LIMIT = 56 * 1024 * 1024


def _cp(*sem):
    return pltpu.CompilerParams(dimension_semantics=sem, vmem_limit_bytes=VMEM_LIMIT)


def _into(dst, kernel, *, in_specs, out_specs, out_shape=(), **kw):
    multi = isinstance(out_specs, (list, tuple))
    fresh = isinstance(dst, jax.ShapeDtypeStruct)
    first = dst if fresh else jax.ShapeDtypeStruct(dst.shape, dst.dtype)
    shapes = [first, *out_shape] if multi else first
    if fresh:
        return pl.pallas_call(kernel, in_specs=in_specs, out_specs=out_specs, out_shape=shapes, **kw)
    n_in = len(in_specs)

    def body(*refs):
        kernel(*refs[:n_in], *refs[n_in + 1:])

    call = pl.pallas_call(body, in_specs=[*in_specs, pl.BlockSpec(memory_space=pl.ANY)],
                          out_specs=out_specs, out_shape=shapes, input_output_aliases={n_in: 0}, **kw)
    return lambda *args: call(*args, dst)


def _split_bf16(x):
    hi = x.astype(BF16)
    lo = (x - hi.astype(F32)).astype(BF16)
    return hi, lo


def _dot(a, b):
    return jnp.dot(a, b, preferred_element_type=F32)


def _dot_nt(a, b):
    return lax.dot_general(a, b, (((1,), (1,)), ((), ())), preferred_element_type=F32)


def _dot3(a, b):
    ah, al = _split_bf16(a)
    bh, bl = _split_bf16(b)
    return _dot(ah, bh) + _dot(al, bh) + _dot(ah, bl)


def _ones_bd(n=LANES, blk=HD_A):
    r = lax.broadcasted_iota(jnp.int32, (n, n), 0) // blk
    c = lax.broadcasted_iota(jnp.int32, (n, n), 1) // blk
    return jnp.where(r == c, 1.0, 0.0).astype(BF16)


def _headsum(x, ones_bd):
    hi, lo = _split_bf16(x)
    outs = []
    for g in range(x.shape[1] // LANES):
        sl = slice(g * LANES, (g + 1) * LANES)
        outs.append(_dot(hi[:, sl], ones_bd) + _dot(lo[:, sl], ones_bd))
    return jnp.concatenate(outs, axis=1) if len(outs) > 1 else outs[0]


def _group_fn(mp, ts):
    def g(row0):
        return jnp.where(row0 < mp, 0, 1 + (row0 - mp) // ts)
    return g


def _mod_kernel(c_ref, w_ref, b_ref, o_ref):
    c = c_ref[...]
    s = c * jax.nn.sigmoid(c)
    o_ref[...] = _dot3(s, w_ref[...]) + b_ref[...]


def _modulation(cv, w_mod, b_mod):
    nl, d, n = w_mod.shape
    tn = 512
    return pl.pallas_call(
        _mod_kernel,
        grid=(nl, n // tn),
        in_specs=[pl.BlockSpec((8, d), lambda l, j: (0, 0)),
                  pl.BlockSpec((None, d, tn), lambda l, j: (l, 0, j)),
                  pl.BlockSpec((None, 1, tn), lambda l, j: (l, 0, j))],
        out_specs=pl.BlockSpec((None, 8, tn), lambda l, j: (l, 0, j)),
        out_shape=jax.ShapeDtypeStruct((nl, 8, n), F32),
        compiler_params=_cp("arbitrary", "arbitrary"),
        name="modulation",
    )(cv, w_mod, b_mod.reshape(nl, 1, n))


def _norm_mod_kernel(x_ref, g_ref, sc_ref, sh_ref, o_ref):
    x = x_ref[...]
    ms = jnp.mean(x * x, axis=-1, keepdims=True)
    y = x * lax.rsqrt(ms + EPS) * g_ref[...]
    o_ref[...] = (y * (1.0 + sc_ref[...]) + sh_ref[...]).astype(o_ref.dtype)


def _norm_mod(x, g_all, mod, l, which_sc, which_sh, grp, tr=256):
    m, d = x.shape
    tr = min(tr, m)
    mspec = lambda w: pl.BlockSpec((None, None, None, 1, d), lambda i: (l, grp(i * tr), w, 0, 0))
    return pl.pallas_call(
        _norm_mod_kernel,
        grid=(m // tr,),
        in_specs=[pl.BlockSpec((tr, d), lambda i: (i, 0)),
                  pl.BlockSpec((None, 1, d), lambda i: (l, 0, 0)),
                  mspec(which_sc), mspec(which_sh)],
        out_specs=pl.BlockSpec((tr, d), lambda i: (i, 0)),
        out_shape=jax.ShapeDtypeStruct((m, d), BF16),
        compiler_params=_cp("arbitrary"),
        name="norm_mod",
    )(x, g_all, mod, mod)


def _rmsnorm_kernel(x_ref, g_ref, o_ref):
    x = x_ref[...]
    ms = jnp.mean(x * x, axis=-1, keepdims=True)
    o_ref[...] = x * lax.rsqrt(ms + EPS) * g_ref[...]


def _rmsnorm(x, g, tr=256):
    m, d = x.shape
    tr = min(tr, m)
    return pl.pallas_call(
        _rmsnorm_kernel,
        grid=(m // tr,),
        in_specs=[pl.BlockSpec((tr, d), lambda i: (i, 0)),
                  pl.BlockSpec((1, d), lambda i: (0, 0))],
        out_specs=pl.BlockSpec((tr, d), lambda i: (i, 0)),
        out_shape=jax.ShapeDtypeStruct((m, d), F32),
        compiler_params=_cp("arbitrary"),
        name="final_norm",
    )(x, g.reshape(1, d))


def _mm_fullk_kernel(*refs, n_a, epilogue):
    a_refs = refs[:n_a]
    w_ref = refs[n_a]
    o_ref = refs[-1]
    acc = None
    k0 = 0
    for a_ref in a_refs:
        kw = a_ref.shape[1]
        part = _dot(a_ref[...], w_ref[k0:k0 + kw, :])
        acc = part if acc is None else acc + part
        k0 += kw
    if epilogue == "relu2":
        acc = jnp.square(jnp.maximum(acc, 0.0))
    elif epilogue == "resid":
        x_ref, gate_ref = refs[n_a + 1], refs[n_a + 2]
        acc = x_ref[...] + gate_ref[...] * acc
    o_ref[...] = acc.astype(o_ref.dtype)


def _mm_fullk(a_list, w_all, l, out_dtype, epilogue="none", resid=None, tm=1024, tn=512, name="mm"):
    m = a_list[0].shape[0]
    k, n = w_all.shape[1], w_all.shape[2]
    tm, tn = min(tm, m), min(tn, n)
    in_specs = [pl.BlockSpec((tm, a.shape[1]), lambda i, j: (i, 0)) for a in a_list]
    in_specs.append(pl.BlockSpec((None, k, tn), lambda i, j: (l, 0, j)))
    args = list(a_list) + [w_all]
    if epilogue == "resid":
        x, mod, which, grp = resid
        d = mod.shape[-1]
        in_specs.append(pl.BlockSpec((tm, tn), lambda i, j: (i, j)))
        in_specs.append(pl.BlockSpec((None, None, None, 1, tn),
                                     lambda i, j: (l, grp(i * tm), which, 0, j)))
        args += [x, mod]
    return pl.pallas_call(
        functools.partial(_mm_fullk_kernel, n_a=len(a_list), epilogue=epilogue),
        grid=(m // tm, n // tn),
        in_specs=in_specs,
        out_specs=pl.BlockSpec((tm, tn), lambda i, j: (i, j)),
        out_shape=jax.ShapeDtypeStruct((m, n), out_dtype),
        compiler_params=_cp("arbitrary", "arbitrary"),
        name=name,
    )(*args)


def _mm_kloop_kernel(a_ref, w_ref, x_ref, gate_ref, o_ref, acc_ref):
    kk = pl.program_id(2)

    @pl.when(kk == 0)
    def _():
        acc_ref[...] = jnp.zeros_like(acc_ref)

    acc_ref[...] += _dot(a_ref[...], w_ref[...])

    @pl.when(kk == pl.num_programs(2) - 1)
    def _():
        o_ref[...] = x_ref[...] + gate_ref[...] * acc_ref[...]


def _mm_kloop_resid(a, w_all, l, x, mod, which, grp, tm=1024, tn=1024, tk=1024):
    m, k = a.shape
    n = w_all.shape[2]
    tm, tn, tk = min(tm, m), min(tn, n), min(tk, k)
    return pl.pallas_call(
        _mm_kloop_kernel,
        grid=(m // tm, n // tn, k // tk),
        in_specs=[pl.BlockSpec((tm, tk), lambda i, j, q: (i, q)),
                  pl.BlockSpec((None, tk, tn), lambda i, j, q: (l, q, j)),
                  pl.BlockSpec((tm, tn), lambda i, j, q: (i, j)),
                  pl.BlockSpec((None, None, None, 1, tn),
                               lambda i, j, q: (l, grp(i * tm), which, 0, j))],
        out_specs=pl.BlockSpec((tm, tn), lambda i, j, q: (i, j)),
        out_shape=jax.ShapeDtypeStruct((m, n), F32),
        scratch_shapes=[pltpu.VMEM((tm, tn), F32)],
        compiler_params=_cp("arbitrary", "arbitrary", "arbitrary"),
        name="ffn_down",
    )(a, w_all, x, mod)


def _softplus(z):
    return jnp.maximum(z, 0.0) + jnp.log(1.0 + jnp.exp(-jnp.abs(z)))


def _rwkv_prep_kernel(k_ref, sm_ref, w0_ref, w2_ref, a0_ref, a2_ref, g2_ref, kk_ref_p, ka_ref_p,
                      kk_o, wf_o, wb_o, bf_o, bb_o, kdf_o, kdb_o, g_o):
    ones_bd = _ones_bd()
    k = k_ref[...]
    kraw = k * kk_ref_p[...]
    ss = _headsum(kraw * kraw, ones_bd)
    kk = kraw / jnp.maximum(jnp.sqrt(ss), 1e-12)
    kk_o[...] = kk
    sm = sm_ref[...]
    wd = jnp.tanh(sm[:, 0:LANES])
    ad = sm[:, LANES:2 * LANES]
    gd = jax.nn.sigmoid(sm[:, 2 * LANES:4 * LANES])
    ka = ka_ref_p[...]
    for d, (w_o, b_o, kd_o) in enumerate(((wf_o, bf_o, kdf_o), (wb_o, bb_o, kdb_o))):
        wl = w0_ref[d:d + 1, :] + _dot3(wd, w2_ref[d])
        wlog = -_softplus(-wl) - 0.5
        w_o[...] = jnp.exp(-jnp.exp(wlog))
        a = jax.nn.sigmoid(a0_ref[d:d + 1, :] + _dot3(ad, a2_ref[d]))
        b_o[...] = kk * a
        kd_o[...] = k * (1.0 + (a - 1.0) * ka)
    g_o[...] = _dot3(gd, g2_ref[...])


def _rwkv_prep(y, row0, m, prm, l, tr=256):
    tr = min(tr, m)
    r0 = row0 // tr
    nsm = y.shape[1] // R_SMALL - 1
    out = jax.ShapeDtypeStruct((m, GW), F32)
    ospec = pl.BlockSpec((tr, GW), lambda i: (i, 0))
    pspec = lambda shape: pl.BlockSpec((None,) + shape, lambda i: (l,) + (0,) * len(shape))
    return pl.pallas_call(
        _rwkv_prep_kernel,
        grid=(m // tr,),
        in_specs=[pl.BlockSpec((tr, GW), lambda i: (r0 + i, 1)),
                  pl.BlockSpec((tr, R_SMALL), lambda i: (r0 + i, nsm)),
                  pspec((2, GW)), pspec((2, LANES, GW)), pspec((2, GW)), pspec((2, LANES, GW)),
                  pspec((2 * LANES, GW)), pspec((1, GW)), pspec((1, GW))],
        out_specs=[ospec] * 8,
        out_shape=[out] * 8,
        compiler_params=_cp("arbitrary"),
        name="rwkv_prep",
    )(y, y, prm["w0"], prm["w2p"], prm["a0"], prm["a2p"], prm["g2p"], prm["k_k"], prm["k_a"])


def _rwkv_scan_kernel(kk_f, r_f, v_f, w_f, b_f, kd_f, kk_b, r_b, v_b, w_b, b_b, kd_b,
                      s0f_ref, s0b_ref, of_ref, ob_ref, sf_ref, sb_ref, st_ref, red_ref, oacc_ref,
                      *, tc, gpc):
    c = pl.program_id(1)

    @pl.when(c == 0)
    def _():
        st_ref[0] = s0f_ref[...]
        st_ref[1] = s0b_ref[...]

    sub = 8
    ngrp = GW // LANES
    cr = gpc * HD_A
    nch = ngrp // gpc
    ones_bd = _ones_bd()
    rowi = lax.broadcasted_iota(jnp.int32, (cr, LANES), 0) % HD_A
    lane = lax.broadcasted_iota(jnp.int32, (cr, LANES), 1) % HD_A
    eye = jnp.where(rowi == lane, 1.0, 0.0).astype(BF16)
    dirs = ((kk_f, r_f, v_f, w_f, b_f, kd_f), (kk_b, r_b, v_b, w_b, b_b, kd_b))

    def bc(ref, base, rj, ch):
        r = ref[pl.ds(base, sub), pl.ds(ch * gpc * LANES, gpc * LANES)][rj:rj + 1, :]
        parts = [jnp.broadcast_to(r[:, q * LANES:(q + 1) * LANES], (HD_A, LANES)) for q in range(gpc)]
        return parts[0] if gpc == 1 else jnp.concatenate(parts, axis=0)

    def here(d, i, j):
        if d == 0:
            return pl.multiple_of(i * sub, sub), j
        return pl.multiple_of(tc - sub - i * sub, sub), sub - 1 - j

    def ahead(d, i, j):
        if j + 1 < sub:
            return here(d, i, j + 1)
        if d == 0:
            return pl.multiple_of(jnp.minimum(i * sub + sub, tc - sub), sub), 0
        return pl.multiple_of(jnp.maximum(tc - 2 * sub - i * sub, 0), sub), sub - 1

    for d, (kk_r, r_r, v_r, w_r, b_r, kd_r) in enumerate(dirs):
        base, rj = here(d, 0, 0)
        for ch in range(nch):
            rws = pl.ds(ch * cr, cr)
            sh = st_ref[d, rws, :].astype(BF16)
            lhs = jnp.concatenate([sh * bc(kk_r, base, rj, ch).astype(BF16),
                                   eye * bc(v_r, base, rj, ch).astype(BF16)], axis=0)
            red_ref[d, ch] = _dot(lhs, ones_bd)

    def step(i, carry):
        red = {(d, ch): red_ref[d, ch] for d in range(2) for ch in range(nch)}
        for j in range(sub):
            for d, (kk_r, r_r, v_r, w_r, b_r, kd_r) in enumerate(dirs):
                base, rj = here(d, i, j)
                nbase, nrj = ahead(d, i, j)
                for ch in range(nch):
                    rws = pl.ds(ch * cr, cr)
                    rd = red[(d, ch)]
                    s = (st_ref[d, rws, :] * bc(w_r, base, rj, ch) - rd[:cr] * bc(b_r, base, rj, ch)
                         + rd[cr:] * bc(kd_r, base, rj, ch))
                    st_ref[d, rws, :] = s
                    sh = s.astype(BF16)
                    lhs = jnp.concatenate([sh * bc(kk_r, nbase, nrj, ch).astype(BF16),
                                           eye * bc(v_r, nbase, nrj, ch).astype(BF16),
                                           sh * bc(r_r, base, rj, ch).astype(BF16)], axis=0)
                    out = _dot(lhs, ones_bd)
                    red[(d, ch)] = out[:2 * cr]
                    pltpu.store(oacc_ref.at[d, rws, :], out[2 * cr:], mask=lane == base + rj)
        for key, val in red.items():
            red_ref[key[0], key[1]] = val
        return carry

    lax.fori_loop(0, tc // sub, step, 0)

    for d, o_ref in enumerate((of_ref, ob_ref)):
        ot = oacc_ref[d].T
        top, bot = ot[:HD_A], ot[HD_A:]
        pieces = []
        for g in range(ngrp):
            pieces += [top[:, g * HD_A:(g + 1) * HD_A], bot[:, g * HD_A:(g + 1) * HD_A]]
        o_ref[...] = jnp.concatenate(pieces, axis=1)

    @pl.when(c == pl.num_programs(1) - 1)
    def _():
        sf_ref[...] = st_ref[0]
        sb_ref[...] = st_ref[1]


def _rwkv_scan(y, row0, nb, t, pre, s0f, s0b, tc=64, gpc=1):
    kk, wf, wb, bf, bb, kdf, kdb = pre
    assert tc == HD_A
    nc = t // tc
    yb = row0 // tc
    rows = GW // LANES * HD_A
    fw = lambda col: pl.BlockSpec((tc, GW), lambda b, c: (yb + b * nc + c, col))
    bw = lambda col: pl.BlockSpec((tc, GW), lambda b, c: (yb + b * nc + nc - 1 - c, col))
    fwp = pl.BlockSpec((tc, GW), lambda b, c: (b * nc + c, 0))
    bwp = pl.BlockSpec((tc, GW), lambda b, c: (b * nc + nc - 1 - c, 0))
    sspec = pl.BlockSpec((None, rows, LANES), lambda b, c: (b, 0, 0))
    oshape = jax.ShapeDtypeStruct((nb * t, GW), F32)
    sshape = jax.ShapeDtypeStruct((nb, rows, LANES), F32)
    return pl.pallas_call(
        functools.partial(_rwkv_scan_kernel, tc=tc, gpc=gpc),
        grid=(nb, nc),
        in_specs=[fwp, fw(0), fw(2), fwp, fwp, fwp, bwp, bw(0), bw(2), bwp, bwp, bwp, sspec, sspec],
        out_specs=[fwp, bwp, sspec, sspec],
        out_shape=[oshape, oshape, sshape, sshape],
        scratch_shapes=[pltpu.VMEM((2, rows, LANES), F32),
                        pltpu.VMEM((2, rows // (gpc * HD_A), 2 * gpc * HD_A, LANES), F32),
                        pltpu.VMEM((2, rows, LANES), F32)],
        compiler_params=_cp("arbitrary", "arbitrary"),
        name="rwkv_scan",
    )(kk, y, y, wf, bf, kdf, kk, y, y, wb, bb, kdb, s0f, s0b)


def _rwkv_post_kernel(of_ref, ob_ref, r_ref, v_ref, kdf_ref, kdb_ref, g_ref, lnw_ref, lnb_ref, rk_ref, o_ref):
    ones_bd = _ones_bd()
    o = of_ref[...] + ob_ref[...]
    mu = _headsum(o, ones_bd) * (1.0 / HD_A)
    dlt = o - mu
    var = _headsum(dlt * dlt, ones_bd) * (1.0 / HD_A)
    on = dlt * lax.rsqrt(var + GN_EPS) * lnw_ref[...] + lnb_ref[...]
    bonus = _headsum(r_ref[...] * (kdf_ref[...] + kdb_ref[...]) * rk_ref[...], ones_bd) * v_ref[...]
    o_ref[...] = ((on + bonus) * g_ref[...]).astype(o_ref.dtype)


def _rwkv_post(of, ob, y, row0, m, kdf, kdb, g, prm, l, dst, tr=256):
    tr = min(tr, m)
    r0 = row0 // tr
    rs = pl.BlockSpec((tr, GW), lambda i: (i, 0))
    ps = pl.BlockSpec((None, 1, GW), lambda i: (l, 0, 0))
    return _into(
        dst, _rwkv_post_kernel,
        grid=(m // tr,),
        in_specs=[rs, rs, pl.BlockSpec((tr, GW), lambda i: (r0 + i, 0)),
                  pl.BlockSpec((tr, GW), lambda i: (r0 + i, 2)), rs, rs, rs, ps, ps, ps],
        out_specs=pl.BlockSpec((tr, GW), lambda i: (r0 + i, 0)),
        compiler_params=_cp("arbitrary"),
        name="rwkv_post",
    )(of, ob, y, y, kdf, kdb, g, prm["ln_w"], prm["ln_b"], prm["r_k"])


def _rwkv_mixer(y, row0, nb, t, prm, l, s0f, s0b, dst):
    m = nb * t
    kk, wf, wb, bf, bb, kdf, kdb, g = _rwkv_prep(y, row0, m, prm, l)
    of, ob, sf, sb = _rwkv_scan(y, row0, nb, t, (kk, wf, wb, bf, bb, kdf, kdb), s0f, s0b)
    ya = _rwkv_post(of, ob, y, row0, m, kdf, kdb, g, prm, l, dst)
    return ya, sf, sb


def _state_in(s):
    b, h, v, k = s.shape
    return s.reshape(b, h // 2, 2, v, k).transpose(0, 1, 3, 2, 4).reshape(b, h // 2 * v, 2 * k)


def _state_out(s):
    b = s.shape[0]
    hp = s.shape[1] // HD_A
    return s.reshape(b, hp, HD_A, 2, HD_A).transpose(0, 1, 3, 2, 4).reshape(b, 2 * hp, HD_A, HD_A)


def _rope_tables(t):
    pos = jnp.arange(t)
    row = (pos // GRID_W).astype(F32)
    col = (pos % GRID_W).astype(F32)
    lane = jnp.arange(LANES)
    dd = lane % DK_B
    half = DK_B // 2
    nf = half // 2
    part = dd // half
    ii = dd % half
    inv = ROPE_BASE ** (-jnp.arange(nf, dtype=F32) / nf)
    p = jnp.where(part[None, :] == 0, row[:, None], col[:, None])
    ang = p * inv[ii % nf][None, :]
    cos, sin = jnp.cos(ang), jnp.sin(ang)
    first = (ii < nf)[None, :]
    return cos, jnp.where(first, -sin, 0.0), jnp.where(first, 0.0, sin), nf


def _rope(x, cos, s1, s2, nf):
    return x * cos + pltpu.roll(x, LANES - nf, 1) * s1 + pltpu.roll(x, nf, 1) * s2


def _lam(lam_ref, lam_init):
    lv = lam_ref[...]
    a = jnp.sum(lv[0:1] * lv[1:2], axis=-1, keepdims=True)
    b = jnp.sum(lv[2:3] * lv[3:4], axis=-1, keepdims=True)
    return jnp.exp(a) - jnp.exp(b) + lam_init


def _diff_ctx_kernel(q_ref, k_ref, v_ref, lam_ref, g_ref, o_ref, ko_ref, vo_ref, *, lam_init):
    lane = lax.broadcasted_iota(jnp.int32, q_ref.shape, 1)
    q = q_ref[...] * (DK_B ** -0.5)
    k = k_ref[...]
    v = v_ref[...]
    ko_ref[...] = k
    vo_ref[...] = v
    kb = k.astype(BF16)
    lam = _lam(lam_ref, lam_init)
    ps = []
    for h in range(2):
        qh = jnp.where((lane // DK_B) == h, q, 0.0).astype(BF16)
        s = _dot_nt(qh, kb)
        e = jnp.exp(s - jnp.max(s, axis=-1, keepdims=True))
        ps.append(e / jnp.sum(e, axis=-1, keepdims=True))
    p = (ps[0] - lam * ps[1]).astype(BF16)
    o = _dot(p, v.astype(BF16))
    ms = jnp.mean(o * o, axis=-1, keepdims=True)
    o_ref[...] = (o * lax.rsqrt(ms + EPS) * g_ref[...] * (1.0 - lam_init)).astype(o_ref.dtype)


def _diff_ctx(y, row0, nb, t, lam_all, g_all, l, lam_init, col0, dst):
    nh = GW // LANES
    rb = row0 // t
    cq, ck, cv = col0 // LANES, (col0 + GW) // LANES, (col0 + 2 * GW) // LANES
    ysp = lambda cb: pl.BlockSpec((t, LANES), lambda b, h: (rb + b, cb + h))
    cache = jax.ShapeDtypeStruct((nb, nh, t, LANES), F32)
    cspec = pl.BlockSpec((None, None, t, LANES), lambda b, h: (b, h, 0, 0))
    return _into(
        dst, functools.partial(_diff_ctx_kernel, lam_init=lam_init),
        grid=(nb, nh),
        in_specs=[ysp(cq), ysp(ck), ysp(cv),
                  pl.BlockSpec((None, 4, DK_B), lambda b, h: (l, 0, 0)),
                  pl.BlockSpec((None, 1, LANES), lambda b, h: (l, 0, 0))],
        out_specs=[pl.BlockSpec((t, LANES), lambda b, h: (rb + b, h)), cspec, cspec],
        out_shape=[cache, cache],
        compiler_params=_cp("arbitrary", "arbitrary"),
        name="diff_ctx",
    )(y, y, y, lam_all, g_all)


def _diff_lat_kernel(q_ref, k_ref, v_ref, ck_ref, cv_ref, cos_ref, s1_ref, s2_ref, lam_ref, g_ref,
                     o_ref, kb_ref, vb_ref, *, lam_init, tq, nf):
    t = q_ref.shape[0]
    cos, s1, s2 = cos_ref[...], s1_ref[...], s2_ref[...]
    kb_ref[...] = _rope(k_ref[...], cos, s1, s2, nf).astype(BF16)
    vb_ref[...] = v_ref[...].astype(BF16)
    ckb = ck_ref[...].astype(BF16)
    cvb = cv_ref[...].astype(BF16)
    lam = _lam(lam_ref, lam_init)
    lane = lax.broadcasted_iota(jnp.int32, (tq, LANES), 1)
    gsc = g_ref[...] * (1.0 - lam_init)

    def blk(i, carry):
        rows = pl.ds(pl.multiple_of(i * tq, tq), tq)
        q = _rope(q_ref[rows, :] * (DK_B ** -0.5), cos_ref[rows, :], s1_ref[rows, :], s2_ref[rows, :], nf)
        pcs, pls = [], []
        for h in range(2):
            qh = jnp.where((lane // DK_B) == h, q, 0.0).astype(BF16)
            sc = _dot_nt(qh, ckb)
            sl = _dot_nt(qh, kb_ref[...])
            mx = jnp.maximum(jnp.max(sc, axis=-1, keepdims=True), jnp.max(sl, axis=-1, keepdims=True))
            ec = jnp.exp(sc - mx)
            el = jnp.exp(sl - mx)
            z = jnp.sum(ec, axis=-1, keepdims=True) + jnp.sum(el, axis=-1, keepdims=True)
            pcs.append(ec / z)
            pls.append(el / z)
        pc = (pcs[0] - lam * pcs[1]).astype(BF16)
        plat = (pls[0] - lam * pls[1]).astype(BF16)
        o = _dot(pc, cvb) + _dot(plat, vb_ref[...])
        ms = jnp.mean(o * o, axis=-1, keepdims=True)
        o_ref[rows, :] = (o * lax.rsqrt(ms + EPS) * gsc).astype(o_ref.dtype)
        return carry

    lax.fori_loop(0, t // tq, blk, 0)


def _diff_lat(y, row0, nb, t, cache_k, cache_v, lam_all, g_all, l, lam_init, col0, dst, tq=256):
    nh = GW // LANES
    rb = row0 // t
    tq = min(tq, t)
    cq, ck, cv = col0 // LANES, (col0 + GW) // LANES, (col0 + 2 * GW) // LANES
    ysp = lambda cb: pl.BlockSpec((t, LANES), lambda b, h: (rb + b, cb + h))
    past = cache_k.shape[3]
    csp = pl.BlockSpec((None, None, None, past, LANES), lambda b, h: (b, l, h, 0, 0))
    cos, s1, s2, nf = _rope_tables(t)
    tsp = pl.BlockSpec((t, LANES), lambda b, h: (0, 0))
    return _into(
        dst, functools.partial(_diff_lat_kernel, lam_init=lam_init, tq=tq, nf=nf),
        grid=(nb, nh),
        in_specs=[ysp(cq), ysp(ck), ysp(cv), csp, csp, tsp, tsp, tsp,
                  pl.BlockSpec((None, 4, DK_B), lambda b, h: (l, 0, 0)),
                  pl.BlockSpec((None, 1, LANES), lambda b, h: (l, 0, 0))],
        out_specs=pl.BlockSpec((t, LANES), lambda b, h: (rb + b, h)),
        scratch_shapes=[pltpu.VMEM((t, LANES), BF16), pltpu.VMEM((t, LANES), BF16)],
        compiler_params=_cp("arbitrary", "arbitrary"),
        name="diff_lat",
    )(y, y, y, cache_k, cache_v, cos, s1, s2, lam_all, g_all)


def _na_ctx_kernel(q_ref, k_ref, v_ref, o_ref):
    lane = lax.broadcasted_iota(jnp.int32, q_ref.shape, 1)
    q = q_ref[...] * (HD_C ** -0.5)
    kb = k_ref[...].astype(BF16)
    vb = v_ref[...].astype(BF16)
    outs = []
    for h in range(2):
        qh = jnp.where((lane // HD_C) == h, q, 0.0).astype(BF16)
        s = _dot_nt(qh, kb)
        e = jnp.exp(s - jnp.max(s, axis=-1, keepdims=True))
        p = e / jnp.sum(e, axis=-1, keepdims=True)
        outs.append(_dot(p.astype(BF16), vb))
    o_ref[...] = jnp.where((lane // HD_C) == 0, outs[0], outs[1]).astype(o_ref.dtype)


def _na_ctx(y, row0, nb, t, col0, dst):
    npair = GW // LANES
    rb = row0 // t
    cq, ck, cv = col0 // LANES, (col0 + GW) // LANES, (col0 + 2 * GW) // LANES
    ysp = lambda cb: pl.BlockSpec((t, LANES), lambda b, p: (rb + b, cb + p))
    return _into(
        dst, _na_ctx_kernel,
        grid=(nb, npair),
        in_specs=[ysp(cq), ysp(ck), ysp(cv)],
        out_specs=pl.BlockSpec((t, LANES), lambda b, p: (rb + b, p)),
        compiler_params=_cp("arbitrary", "arbitrary"),
        name="na_ctx",
    )(y, y, y)


def _na_bias_kernel(rpb_ref, o_ref, t_ref):
    ww = lax.broadcasted_iota(jnp.int32, (GRID_W, GRID_W), 0)
    jj = lax.broadcasted_iota(jnp.int32, (GRID_W, GRID_W), 1)
    c0 = jnp.clip(ww - WIN_C // 2, 0, GRID_W - WIN_C)
    allowed = (jj >= c0) & (jj < c0 + WIN_C)
    diff = jj - ww + (WIN_C - 1)
    for dr in range(2 * WIN_R - 1):
        acc = jnp.full((GRID_W, GRID_W), NEG, F32)
        for dc in range(2 * WIN_C - 1):
            acc = jnp.where(diff == dc, rpb_ref[dr, dc], acc)
        t_ref[dr] = jnp.where(allowed, acc, NEG)
    for pat in range(WIN_R):
        for i in range(WIN_R):
            o_ref[pat, :, i * GRID_W:(i + 1) * GRID_W] = t_ref[i - pat + WIN_R - 1]


def _na_bias_table(rpb):
    nl, nh, ndr, ndc = rpb.shape
    return pl.pallas_call(
        _na_bias_kernel,
        grid=(nl, nh),
        in_specs=[pl.BlockSpec((None, None, ndr, ndc), lambda l, h: (l, h, 0, 0), memory_space=pltpu.SMEM)],
        out_specs=pl.BlockSpec((None, None, WIN_R, GRID_W, WIN_R * GRID_W), lambda l, h: (l, h, 0, 0, 0)),
        out_shape=jax.ShapeDtypeStruct((nl, nh, WIN_R, GRID_W, WIN_R * GRID_W), F32),
        scratch_shapes=[pltpu.VMEM((ndr, GRID_W, GRID_W), F32)],
        compiler_params=_cp("arbitrary", "arbitrary"),
        name="na_bias",
    )(rpb)


def _na_lat_kernel(q_ref, k_ref, v_ref, ck_ref, cv_ref, bias_ref, o_ref, kb_ref, vb_ref):
    t = q_ref.shape[0]
    rows = t // GRID_W
    kr = min(WIN_R, rows)
    kb_ref[...] = k_ref[...].astype(BF16)
    vb_ref[...] = v_ref[...].astype(BF16)
    ckb = ck_ref[...].astype(BF16)
    cvb = cv_ref[...].astype(BF16)
    lane = lax.broadcasted_iota(jnp.int32, (GRID_W, LANES), 1)

    def row_step(r, carry):
        r0 = jnp.clip(r - kr // 2, 0, rows - kr)
        pat = r - r0
        qrows = pl.ds(pl.multiple_of(r * GRID_W, GRID_W), GRID_W)
        wrows = pl.ds(pl.multiple_of(r0 * GRID_W, GRID_W), kr * GRID_W)
        q = q_ref[qrows, :] * (HD_C ** -0.5)
        kw = kb_ref[wrows, :]
        vw = vb_ref[wrows, :]
        outs = []
        for h in range(2):
            qh = jnp.where((lane // HD_C) == h, q, 0.0).astype(BF16)
            sw = _dot_nt(qh, kw) + bias_ref[h, pat]
            sc = _dot_nt(qh, ckb)
            mx = jnp.maximum(jnp.max(sw, axis=-1, keepdims=True), jnp.max(sc, axis=-1, keepdims=True))
            ew = jnp.exp(sw - mx)
            ec = jnp.exp(sc - mx)
            z = jnp.sum(ew, axis=-1, keepdims=True) + jnp.sum(ec, axis=-1, keepdims=True)
            outs.append(_dot((ew / z).astype(BF16), vw) + _dot((ec / z).astype(BF16), cvb))
        o_ref[qrows, :] = jnp.where((lane // HD_C) == 0, outs[0], outs[1]).astype(o_ref.dtype)
        return carry

    lax.fori_loop(0, rows, row_step, 0)


def _na_lat(y, row0, nb, t, ctx_k, ctx_v, bias, l, col0, dst):
    npair = GW // LANES
    rb = row0 // t
    cq, ck, cv = col0 // LANES, (col0 + GW) // LANES, (col0 + 2 * GW) // LANES
    ysp = lambda cb: pl.BlockSpec((t, LANES), lambda b, p: (rb + b, cb + p))
    past = ctx_k.shape[3]
    csp = pl.BlockSpec((None, None, None, past, LANES), lambda b, p: (b, l, p, 0, 0))
    bsp = pl.BlockSpec((None, 2, WIN_R, GRID_W, WIN_R * GRID_W), lambda b, p: (l, p, 0, 0, 0))
    return _into(
        dst, _na_lat_kernel,
        grid=(nb, npair),
        in_specs=[ysp(cq), ysp(ck), ysp(cv), csp, csp, bsp],
        out_specs=pl.BlockSpec((t, LANES), lambda b, p: (rb + b, p)),
        scratch_shapes=[pltpu.VMEM((t, LANES), BF16), pltpu.VMEM((t, LANES), BF16)],
        compiler_params=_cp("arbitrary", "arbitrary"),
        name="na_lat",
    )(y, y, y, ctx_k, ctx_v, bias)


def _hy_positions(length):
    t = jnp.linspace(0.0, 1.0, length, dtype=F32)[:, None]
    ang = 2.0 * math.pi * jnp.arange(length, dtype=F32) / length
    bands = jnp.linspace(1e-4, HY_BANDS - 1, HY_BANDS, dtype=F32)
    z = jnp.concatenate([t, jnp.cos(ang[:, None] * bands[None, :]),
                         -jnp.sin(ang[:, None] * bands[None, :])], axis=-1)
    z = jnp.pad(z, ((0, 0), (0, LANES - z.shape[1])))
    mn = math.log(HY_TARGET) / HY_SLOW
    mx = math.log(HY_TARGET) / HY_FAST
    deltas = jnp.abs(jnp.linspace(mn, mx, GW, dtype=F32))
    decay = jnp.exp(-t * deltas[None, :])
    return z, decay


def _hy_filter_kernel(z_ref, dec_ref, w1_ref, b1_ref, f1_ref, w2_ref, b2_ref, f2_ref, w3_ref, o_ref):
    hid = jnp.sin(f1_ref[...] * (_dot3(z_ref[...], w1_ref[...]) + b1_ref[...]))
    hid = jnp.sin(f2_ref[...] * (_dot3(hid, w2_ref[...]) + b2_ref[...]))
    h = _dot3(hid, w3_ref[...])
    dec = dec_ref[...]
    first = (lax.broadcasted_iota(jnp.int32, dec.shape, 0) + pl.program_id(0) * dec.shape[0]) == 0
    for o in range(h.shape[1] // (2 * GW)):
        hf = h[:, (2 * o) * GW:(2 * o + 1) * GW] * dec
        hb = jnp.where(first, 0.0, h[:, (2 * o + 1) * GW:(2 * o + 2) * GW] * dec)
        o_ref[:, (2 * o) * GW:(2 * o + 1) * GW] = hf + hb
        o_ref[:, (2 * o + 1) * GW:(2 * o + 2) * GW] = hf - hb


def _hy_filters(length, prm, l):
    z, decay = _hy_positions(length)
    n3 = prm["w3"].shape[2]
    hid = prm["w2"].shape[1]
    tl = min(length, 256)
    ps = lambda shape: pl.BlockSpec((None,) + shape, lambda i: (l,) + (0,) * len(shape))
    return pl.pallas_call(
        _hy_filter_kernel,
        grid=(length // tl,),
        in_specs=[pl.BlockSpec((tl, LANES), lambda i: (i, 0)), pl.BlockSpec((tl, GW), lambda i: (i, 0)),
                  ps((LANES, hid)), ps((1, hid)), ps((1, hid)), ps((hid, hid)), ps((1, hid)), ps((1, hid)),
                  ps((hid, n3))],
        out_specs=pl.BlockSpec((tl, n3), lambda i: (i, 0)),
        out_shape=jax.ShapeDtypeStruct((length, n3), F32),
        compiler_params=_cp("arbitrary"),
        name="hy_filter",
    )(z, decay, prm["w1p"], prm["b1"], prm["f1"], prm["w2"], prm["b2"], prm["f2"], prm["w3"])


def _dft_mats(length):
    n2 = 2 * length
    k = jnp.arange(length, dtype=jnp.int32)
    kn = (k[:, None] * k[None, :]) % n2
    ang = kn.astype(F32) * (2.0 * math.pi / n2)
    c, s = jnp.cos(ang), jnp.sin(ang)
    alt = jnp.where(k % 2 == 0, 1.0, -1.0).astype(F32)
    f_im = jnp.where((k == 0)[:, None], alt[None, :], -s)
    fwd = jnp.concatenate([c, f_im], axis=0).astype(BF16)
    ck = jnp.where(k == 0, 1.0, 2.0)[None, :]
    g_re = c * ck / n2
    g_im = jnp.where((k == 0)[None, :], alt[:, None] / n2, -2.0 * s / n2)
    inv = jnp.concatenate([g_re, g_im], axis=1).astype(BF16)
    return fwd, inv


def _mm_plain_kernel(a_ref, w_ref, o_ref):
    o_ref[...] = _dot(a_ref[...], w_ref[...].astype(BF16))


def _hy_spectrum(fwd, hk, tm=512, tn=512):
    m, k = fwd.shape
    n = hk.shape[1]
    tm, tn = min(tm, m), min(tn, n)
    return pl.pallas_call(
        _mm_plain_kernel,
        grid=(m // tm, n // tn),
        in_specs=[pl.BlockSpec((tm, k), lambda i, j: (i, 0)), pl.BlockSpec((k, tn), lambda i, j: (0, j))],
        out_specs=pl.BlockSpec((tm, tn), lambda i, j: (i, j)),
        out_shape=jax.ShapeDtypeStruct((m, n), F32),
        compiler_params=_cp("arbitrary", "arbitrary"),
        name="hy_spectrum",
    )(fwd, hk)


def _short_conv_kernel(u_ref, w_ref, b_ref, o_ref):
    u = u_ref[...]
    t = u.shape[0]
    rowi = lax.broadcasted_iota(jnp.int32, u.shape, 0)
    prev = jnp.where(rowi == 0, 0.0, pltpu.roll(u, 1, 0))
    nxt = jnp.where(rowi == t - 1, 0.0, pltpu.roll(u, t - 1, 0))
    o_ref[...] = prev * w_ref[0:1, :] + u * w_ref[1:2, :] + nxt * w_ref[2:3, :] + b_ref[...]


def _short_conv(y, row0, nb, t, w_all, b_all, l, col0, width, tc=512):
    rb = row0 // t
    cb = col0 // tc
    return pl.pallas_call(
        _short_conv_kernel,
        grid=(nb, width // tc),
        in_specs=[pl.BlockSpec((t, tc), lambda b, j: (rb + b, cb + j)),
                  pl.BlockSpec((None, 3, tc), lambda b, j: (l, 0, j)),
                  pl.BlockSpec((None, 1, tc), lambda b, j: (l, 0, j))],
        out_specs=pl.BlockSpec((t, tc), lambda b, j: (b, j)),
        out_shape=jax.ShapeDtypeStruct((nb * t, width), F32),
        compiler_params=_cp("arbitrary", "arbitrary"),
        name="short_conv",
    )(y, w_all, b_all)


def _hy_fwd_kernel(fre_ref, fim_ref, z_ref, kre_ref, kny_ref, kim_ref, o_ref):
    zb = z_ref[...].astype(BF16)
    re = _dot(fre_ref[...], zb)
    im = _dot(fim_ref[...], zb)
    tm = re.shape[0]
    dc = (lax.broadcasted_iota(jnp.int32, re.shape, 0) + pl.program_id(1) * tm) == 0
    kre = kre_ref[...]
    kim = jnp.where(dc, kny_ref[...], kim_ref[...])
    o_ref[0] = jnp.where(dc, re * kre, re * kre - im * kim).astype(o_ref.dtype)
    o_ref[1] = jnp.where(dc, im * kim, re * kim + im * kre).astype(o_ref.dtype)


def _hy_fwd(fwd, z, zcol, nb, t, kspec, o, tm=512, tn=512):
    tm, tn = min(tm, t), min(tn, GW)
    ni = t // tm
    zc = zcol // tn
    cs, cd = (2 * o) * GW // tn, (2 * o + 1) * GW // tn
    return pl.pallas_call(
        _hy_fwd_kernel,
        grid=(nb, ni, GW // tn),
        in_specs=[pl.BlockSpec((tm, t), lambda b, i, j: (i, 0)),
                  pl.BlockSpec((tm, t), lambda b, i, j: (ni + i, 0)),
                  pl.BlockSpec((t, tn), lambda b, i, j: (b, zc + j)),
                  pl.BlockSpec((tm, tn), lambda b, i, j: (i, cs + j)),
                  pl.BlockSpec((tm, tn), lambda b, i, j: (ni + i, cs + j)),
                  pl.BlockSpec((tm, tn), lambda b, i, j: (ni + i, cd + j))],
        out_specs=pl.BlockSpec((None, 2, tm, tn), lambda b, i, j: (b, 0, i, j)),
        out_shape=jax.ShapeDtypeStruct((nb, 2, t, GW), BF16),
        compiler_params=_cp("arbitrary", "arbitrary", "arbitrary"),
        name="hy_fwd",
    )(fwd, fwd, z, kspec, kspec, kspec)


def _hy_inv_kernel(g_ref, s_ref, zin_ref, gate_ref, skip_ref, o_ref):
    yv = _dot(g_ref[...], s_ref[...])
    o_ref[...] = (gate_ref[...] * (yv + skip_ref[...] * zin_ref[...])).astype(o_ref.dtype)


def _hy_inv(inv, spec, z, zcol, gate, gcol, skip_all, l, o, nb, t, dst, row0=0, tm=512, tn=512):
    tm, tn = min(tm, t), min(tn, GW)
    ni = t // tm
    zc, gc = zcol // tn, gcol // tn
    ro = row0 // tm
    return _into(
        dst, _hy_inv_kernel,
        grid=(nb, ni, GW // tn),
        in_specs=[pl.BlockSpec((tm, 2 * t), lambda b, i, j: (i, 0)),
                  pl.BlockSpec((None, 2 * t, tn), lambda b, i, j: (b, 0, j)),
                  pl.BlockSpec((tm, tn), lambda b, i, j: (b * ni + i, zc + j)),
                  pl.BlockSpec((tm, tn), lambda b, i, j: (b * ni + i, gc + j)),
                  pl.BlockSpec((None, None, 1, tn), lambda b, i, j: (l, o, 0, j))],
        out_specs=pl.BlockSpec((tm, tn), lambda b, i, j: (ro + b * ni + i, j)),
        compiler_params=_cp("arbitrary", "arbitrary", "arbitrary"),
        name="hy_inv",
    )(inv, spec.reshape(nb, 2 * t, GW), z, gate, skip_all)


def _hyena_mixer(y, row0, nb, t, prm, l, col0, consts, dst):
    fwd, inv, kspec = consts
    u = _short_conv(y, row0, nb, t, prm["sconv_w"], prm["sconv_b"], l, col0, 3 * GW)
    spec = _hy_fwd(fwd, u, 0, nb, t, kspec, 0)
    z = _hy_inv(inv, spec, u, 0, u, GW, prm["skip"], l, 0, nb, t, jax.ShapeDtypeStruct((nb * t, GW), F32))
    spec = _hy_fwd(fwd, z, 0, nb, t, kspec, 1)
    return _hy_inv(inv, spec, z, 0, u, 2 * GW, prm["skip"], l, 1, nb, t, dst, row0)


def _hyena_consts(t, prm, l):
    fwd, inv = _dft_mats(t)
    return fwd, inv, _hy_spectrum(fwd, _hy_filters(t, prm, l))


def _permute_w_in(w_in):
    gw = GW
    big = w_in[..., :3 * gw]
    o = 3 * gw
    small = w_in[..., o:o + 416]
    rest = w_in[..., o + 416:]
    pad = jnp.zeros(w_in.shape[:-1] + (R_SMALL - 416,), w_in.dtype)
    return jnp.concatenate([big, rest, small, pad], axis=-1).astype(BF16)


def _pad_rows(w, rows, at=0):
    pad = [(0, 0)] * (w.ndim - 2) + [(at, rows - at - w.shape[-2]), (0, 0)]
    return jnp.pad(w, pad)


def kernel(x_prompt, x_sample, cache_diff_k, cache_diff_v, cache_na_k, cache_na_v, state_rwkv_fwd,
           state_rwkv_bwd, c, c_ctx, norm1_g, norm2_g, w_mod, b_mod, w_in, rwkv_w0, rwkv_w2, rwkv_a0,
           rwkv_a2, rwkv_g2, rwkv_k_k, rwkv_k_a, rwkv_r_k, rwkv_ln_w, rwkv_ln_b, diff_lam,
           diff_subln_g, na_rpb, hy_sconv_w, hy_sconv_b, hy_f_w1, hy_f_b1, hy_f_freq1, hy_f_w2,
           hy_f_b2, hy_f_freq2, hy_f_w3, hy_skip, w_out, w_up, w_down, final_g):
    bp, tp, d = x_prompt.shape
    bs, ts, _ = x_sample.shape
    nl = w_in.shape[0]
    mp, ms = bp * tp, bs * ts
    assert mp % 1024 == 0 and ts % 1024 == 0 and bs + 1 <= 8
    grp = _group_fn(mp, ts)

    x = jnp.concatenate([x_prompt.reshape(mp, d), x_sample.reshape(ms, d)], axis=0)
    cv = jnp.concatenate([c_ctx[None], c, jnp.zeros((8 - 1 - bs, d), F32)], axis=0)
    mod = _modulation(cv, w_mod, b_mod).reshape(nl, 8, N_MOD, 1, d)

    w_in_b = _permute_w_in(w_in)
    w_out_b = w_out.astype(BF16)
    w_up_b = w_up.astype(BF16)
    w_down_b = w_down.astype(BF16)
    row = lambda p: p.reshape(nl, 1, -1)
    rw = {
        "w0": rwkv_w0, "a0": rwkv_a0,
        "w2p": jnp.stack([_pad_rows(rwkv_w2[:, 0], LANES, 0), _pad_rows(rwkv_w2[:, 1], LANES, 64)], axis=1),
        "a2p": jnp.stack([_pad_rows(rwkv_a2[:, 0], LANES, 0), _pad_rows(rwkv_a2[:, 1], LANES, 64)], axis=1),
        "g2p": _pad_rows(rwkv_g2, 2 * LANES, 0),
        "k_k": row(rwkv_k_k), "k_a": row(rwkv_k_a), "r_k": row(rwkv_r_k),
        "ln_w": row(rwkv_ln_w), "ln_b": row(rwkv_ln_b),
    }
    hy = {
        "sconv_w": hy_sconv_w, "sconv_b": hy_sconv_b.reshape(nl, 1, -1),
        "w1p": _pad_rows(hy_f_w1, LANES, 0), "b1": row(hy_f_b1), "f1": row(hy_f_freq1),
        "w2": hy_f_w2, "b2": row(hy_f_b2), "f2": row(hy_f_freq2), "w3": hy_f_w3,
        "skip": hy_skip.reshape(nl, 2, 1, GW),
    }
    subln = diff_subln_g.reshape(nl, 1, -1)
    norm1 = norm1_g.reshape(nl, 1, d)
    norm2 = norm2_g.reshape(nl, 1, d)
    na_bias = _na_bias_table(na_rpb)
    pair = lambda a: a.reshape(a.shape[0], a.shape[1], a.shape[2] // 2, 2, a.shape[3], a.shape[4]) \
        .transpose(0, 1, 2, 4, 3, 5).reshape(a.shape[0], a.shape[1], a.shape[2] // 2, a.shape[3], 2 * a.shape[4])
    na_ck, na_cv = pair(cache_na_k), pair(cache_na_v)
    zero_state = jnp.zeros((bp, GW // LANES * HD_A, LANES), F32)

    c_b, c_c, c_d = 3 * GW, 6 * GW, 9 * GW
    new = [[] for _ in range(6)]
    for l in range(nl):
        lam_init = 0.8 - 0.6 * math.exp(-0.3 * l)
        h = _norm_mod(x, norm1, mod, l, 1, 0, grp)
        y = _mm_fullk([h], w_in_b, l, F32, name="proj_in")

        fresh = jax.ShapeDtypeStruct((mp + ms, GW), BF16)
        ya, sf, sb = _rwkv_mixer(y, 0, bp, tp, rw, l, zero_state, zero_state, fresh)
        yb, dk, dv = _diff_ctx(y, 0, bp, tp, diff_lam, subln, l, lam_init, c_b, fresh)
        yc = _na_ctx(y, 0, bp, tp, c_c, fresh)
        yd = _hyena_mixer(y, 0, bp, tp, hy, l, c_d, _hyena_consts(tp, hy, l), fresh)
        ya, _, _ = _rwkv_mixer(y, mp, bs, ts, rw, l, _state_in(state_rwkv_fwd[:, l]),
                               _state_in(state_rwkv_bwd[:, l]), ya)
        yb = _diff_lat(y, mp, bs, ts, cache_diff_k, cache_diff_v, diff_lam, subln, l, lam_init, c_b, yb)
        yc = _na_lat(y, mp, bs, ts, na_ck, na_cv, na_bias, l, c_c, yc)
        yd = _hyena_mixer(y, mp, bs, ts, hy, l, c_d, _hyena_consts(ts, hy, l), yd)

        x = _mm_fullk([ya, yb, yc, yd], w_out_b, l, F32, epilogue="resid", resid=(x, mod, 2, grp),
                      name="proj_out")
        h = _norm_mod(x, norm2, mod, l, 4, 3, grp)
        u = _mm_fullk([h], w_up_b, l, BF16, epilogue="relu2", name="ffn_up")
        x = _mm_kloop_resid(u, w_down_b, l, x, mod, 5, grp)

        yp = y[:mp]
        hc = lambda cols, nh, hd: cols.reshape(bp, tp, nh, hd).transpose(0, 2, 1, 3)
        new[0].append(dk)
        new[1].append(dv)
        new[2].append(hc(yp[:, c_c + GW:c_c + 2 * GW], GW // HD_C, HD_C))
        new[3].append(hc(yp[:, c_c + 2 * GW:c_c + 3 * GW], GW // HD_C, HD_C))
        new[4].append(_state_out(sf))
        new[5].append(_state_out(sb))

    yout = _rmsnorm(x, final_g)
    outs = [yout[:mp].reshape(bp, tp, d), yout[mp:].reshape(bs, ts, d)]
    outs += [jnp.stack(n, axis=1) for n in new]
    return tuple(outs)
```

```python
import functools
import math

import jax
import jax.numpy as jnp
import numpy as np
from jax import lax
from jax.experimental import pallas as pl
from jax.experimental.pallas import tpu as pltpu

F32 = jnp.float32
BF16 = jnp.bfloat16

GRID_W = 64
HD_A = 64
DK_B = 64
HD_C = 64
WIN_R = 8
WIN_C = 16
GW = 1024
R_SMALL = 512
N_MOD = 6
GN_EPS = 64e-5
EPS = 1e-6
ROPE_BASE = 10000.0
HY_BANDS = 16
HY_TARGET = 1e-2
HY_FAST = 0.3
HY_SLOW = 1.5
NEG = -1e30
LANES = 128
VMEM_LIMIT = 56 * 1024 * 1024


def _cp(*sem):
    return pltpu.CompilerParams(dimension_semantics=sem, vmem_limit_bytes=VMEM_LIMIT)


def _into(dst, kernel, *, in_specs, out_specs, **kw):
    multi = isinstance(out_specs, (list, tuple))
    dsts = list(dst) if multi else [dst]
    shapes = [d if isinstance(d, jax.ShapeDtypeStruct) else jax.ShapeDtypeStruct(d.shape, d.dtype) for d in dsts]
    held = [(o, d) for o, d in enumerate(dsts) if not isinstance(d, jax.ShapeDtypeStruct)]
    n_in = len(in_specs)

    def body(*refs):
        kernel(*refs[:n_in], *refs[n_in + len(held):])

    call = pl.pallas_call(body, in_specs=[*in_specs] + [pl.BlockSpec(memory_space=pl.ANY)] * len(held),
                          out_specs=out_specs, out_shape=shapes if multi else shapes[0],
                          input_output_aliases={n_in + i: o for i, (o, _) in enumerate(held)}, **kw)
    return lambda *args: call(*args, *[d for _, d in held])


def _split_bf16(x):
    hi = x.astype(BF16)
    lo = (x - hi.astype(F32)).astype(BF16)
    return hi, lo


def _dot(a, b):
    return jnp.dot(a, b, preferred_element_type=F32)


def _dot_nt(a, b):
    return lax.dot_general(a, b, (((1,), (1,)), ((), ())), preferred_element_type=F32)


def _dot3(a, b):
    ah, al = _split_bf16(a)
    bh, bl = _split_bf16(b)
    return _dot(ah, bh) + _dot(al, bh) + _dot(ah, bl)


def _ones_bd(n=LANES, blk=HD_A):
    r = lax.broadcasted_iota(jnp.int32, (n, n), 0) // blk
    c = lax.broadcasted_iota(jnp.int32, (n, n), 1) // blk
    return jnp.where(r == c, 1.0, 0.0).astype(BF16)


def _headsum(x, ones_bd):
    hi, lo = _split_bf16(x)
    outs = []
    for g in range(x.shape[1] // LANES):
        sl = slice(g * LANES, (g + 1) * LANES)
        outs.append(_dot(hi[:, sl], ones_bd) + _dot(lo[:, sl], ones_bd))
    return jnp.concatenate(outs, axis=1) if len(outs) > 1 else outs[0]


def _group_fn(mp, ts):
    def g(row0):
        return jnp.where(row0 < mp, 0, 1 + (row0 - mp) // ts)
    return g


def _mod_kernel(c_ref, w_ref, b_ref, o_ref):
    c = c_ref[...]
    s = c * jax.nn.sigmoid(c)
    o_ref[...] = _dot3(s, w_ref[...]) + b_ref[...]


def _modulation(cv, w_mod, b_mod):
    nl, d, n = w_mod.shape
    tn = 512
    return pl.pallas_call(
        _mod_kernel,
        grid=(nl, n // tn),
        in_specs=[pl.BlockSpec((8, d), lambda l, j: (0, 0)),
                  pl.BlockSpec((None, d, tn), lambda l, j: (l, 0, j)),
                  pl.BlockSpec((None, 1, tn), lambda l, j: (l, 0, j))],
        out_specs=pl.BlockSpec((None, 8, tn), lambda l, j: (l, 0, j)),
        out_shape=jax.ShapeDtypeStruct((nl, 8, n), F32),
        compiler_params=_cp("arbitrary", "arbitrary"),
        name="modulation",
    )(cv, w_mod, b_mod.reshape(nl, 1, n))


def _norm_mod_kernel(x_ref, g_ref, sc_ref, sh_ref, o_ref):
    x = x_ref[...]
    ms = jnp.mean(x * x, axis=-1, keepdims=True)
    y = x * lax.rsqrt(ms + EPS) * g_ref[...]
    o_ref[...] = (y * (1.0 + sc_ref[...]) + sh_ref[...]).astype(o_ref.dtype)


def _norm_mod(x, xrow0, m, g_all, mod, l, which_sc, which_sh, grp, dst, orow0, tr=256):
    d = x.shape[1]
    tr = min(tr, m)
    xr, orr = xrow0 // tr, orow0 // tr
    mspec = lambda w: pl.BlockSpec((None, None, None, 1, d), lambda i: (l, grp(orow0 + i * tr), w, 0, 0))
    return _into(
        dst, _norm_mod_kernel,
        grid=(m // tr,),
        in_specs=[pl.BlockSpec((tr, d), lambda i: (xr + i, 0)),
                  pl.BlockSpec((None, 1, d), lambda i: (l, 0, 0)),
                  mspec(which_sc), mspec(which_sh)],
        out_specs=pl.BlockSpec((tr, d), lambda i: (orr + i, 0)),
        compiler_params=_cp("arbitrary"),
        name="norm_mod",
    )(x, g_all, mod, mod)


def _rmsnorm_kernel(x_ref, g_ref, o_ref):
    x = x_ref[...]
    ms = jnp.mean(x * x, axis=-1, keepdims=True)
    o_ref[...] = x * lax.rsqrt(ms + EPS) * g_ref[...]


def _rmsnorm(x, xrow0, m, g, tr=256):
    d = x.shape[1]
    tr = min(tr, m)
    xr = xrow0 // tr
    return pl.pallas_call(
        _rmsnorm_kernel,
        grid=(m // tr,),
        in_specs=[pl.BlockSpec((tr, d), lambda i: (xr + i, 0)),
                  pl.BlockSpec((1, d), lambda i: (0, 0))],
        out_specs=pl.BlockSpec((tr, d), lambda i: (i, 0)),
        out_shape=jax.ShapeDtypeStruct((m, d), F32),
        compiler_params=_cp("arbitrary"),
        name="final_norm",
    )(x, g.reshape(1, d))


def _mm_fullk_kernel(*refs, n_a, epilogue):
    a_refs = refs[:n_a]
    w_ref = refs[n_a]
    o_ref = refs[-1]
    acc = None
    k0 = 0
    for a_ref in a_refs:
        kw = a_ref.shape[1]
        part = _dot(a_ref[...], w_ref[k0:k0 + kw, :])
        acc = part if acc is None else acc + part
        k0 += kw
    if epilogue == "relu2":
        acc = jnp.square(jnp.maximum(acc, 0.0))
    elif epilogue == "resid":
        x_ref, gate_ref = refs[n_a + 1], refs[n_a + 2]
        acc = x_ref[...] + gate_ref[...] * acc
    o_ref[...] = acc.astype(o_ref.dtype)


def _mm_fullk(a_list, w_all, l, dst, row0, m, epilogue="none", resid=None, tm=1024, tn=512, name="mm"):
    k, n = w_all.shape[1], w_all.shape[2]
    tm, tn = min(tm, m), min(tn, n)
    rb = row0 // tm
    in_specs = [pl.BlockSpec((tm, a.shape[1]), lambda i, j: (rb + i, 0)) for a in a_list]
    in_specs.append(pl.BlockSpec((None, k, tn), lambda i, j: (l, 0, j)))
    args = list(a_list) + [w_all]
    if epilogue == "resid":
        x, xrow0, mod, which, grp = resid
        xb = xrow0 // tm
        in_specs.append(pl.BlockSpec((tm, tn), lambda i, j: (xb + i, j)))
        in_specs.append(pl.BlockSpec((None, None, None, 1, tn),
                                     lambda i, j: (l, grp(row0 + i * tm), which, 0, j)))
        args += [x, mod]
    return _into(
        dst, functools.partial(_mm_fullk_kernel, n_a=len(a_list), epilogue=epilogue),
        grid=(m // tm, n // tn),
        in_specs=in_specs,
        out_specs=pl.BlockSpec((tm, tn), lambda i, j: (rb + i, j)),
        compiler_params=_cp("arbitrary", "arbitrary"),
        name=name,
    )(*args)


def _mm_kloop_kernel(a_ref, w_ref, x_ref, gate_ref, o_ref, acc_ref):
    kk = pl.program_id(2)

    @pl.when(kk == 0)
    def _():
        acc_ref[...] = jnp.zeros_like(acc_ref)

    acc_ref[...] += _dot(a_ref[...], w_ref[...])

    @pl.when(kk == pl.num_programs(2) - 1)
    def _():
        o_ref[...] = x_ref[...] + gate_ref[...] * acc_ref[...]


def _mm_kloop_resid(a, w_all, l, x, mod, which, grp, tm=1024, tn=1024, tk=2048):
    m, k = a.shape
    n = w_all.shape[2]
    tm, tn, tk = min(tm, m), min(tn, n), min(tk, k)
    return pl.pallas_call(
        _mm_kloop_kernel,
        grid=(m // tm, n // tn, k // tk),
        in_specs=[pl.BlockSpec((tm, tk), lambda i, j, q: (i, q)),
                  pl.BlockSpec((None, tk, tn), lambda i, j, q: (l, q, j)),
                  pl.BlockSpec((tm, tn), lambda i, j, q: (i, j)),
                  pl.BlockSpec((None, None, None, 1, tn),
                               lambda i, j, q: (l, grp(i * tm), which, 0, j))],
        out_specs=pl.BlockSpec((tm, tn), lambda i, j, q: (i, j)),
        out_shape=jax.ShapeDtypeStruct((m, n), F32),
        scratch_shapes=[pltpu.VMEM((tm, tn), F32)],
        compiler_params=_cp("arbitrary", "arbitrary", "arbitrary"),
        name="ffn_down",
    )(a, w_all, x, mod)


def _softplus(z):
    return jnp.maximum(z, 0.0) + jnp.log(1.0 + jnp.exp(-jnp.abs(z)))


def _rwkv_prep_kernel(k_ref, sm_ref, w0_ref, w2_ref, a0_ref, a2_ref, g2_ref, kk_ref_p, ka_ref_p,
                      kk_o, wf_o, wb_o, bf_o, bb_o, kdf_o, kdb_o, g_o):
    ones_bd = _ones_bd()
    k = k_ref[...]
    kraw = k * kk_ref_p[...]
    ss = _headsum(kraw * kraw, ones_bd)
    kk = kraw / jnp.maximum(jnp.sqrt(ss), 1e-12)
    kk_o[...] = kk
    sm = sm_ref[...]
    wd = jnp.tanh(sm[:, 0:LANES])
    ad = sm[:, LANES:2 * LANES]
    gd = jax.nn.sigmoid(sm[:, 2 * LANES:4 * LANES])
    ka = ka_ref_p[...]
    for d, (w_o, b_o, kd_o) in enumerate(((wf_o, bf_o, kdf_o), (wb_o, bb_o, kdb_o))):
        wl = w0_ref[d:d + 1, :] + _dot3(wd, w2_ref[d])
        wlog = -_softplus(-wl) - 0.5
        w_o[...] = jnp.exp(-jnp.exp(wlog))
        a = jax.nn.sigmoid(a0_ref[d:d + 1, :] + _dot3(ad, a2_ref[d]))
        b_o[...] = kk * a
        kd_o[...] = k * (1.0 + (a - 1.0) * ka)
    g_o[...] = _dot3(gd, g2_ref[...])


def _rwkv_prep(y, row0, m, prm, l, tr=256):
    tr = min(tr, m)
    r0 = row0 // tr
    nsm = y.shape[1] // R_SMALL - 1
    out = jax.ShapeDtypeStruct((m, GW), F32)
    ospec = pl.BlockSpec((tr, GW), lambda i: (i, 0))
    pspec = lambda shape: pl.BlockSpec((None,) + shape, lambda i: (l,) + (0,) * len(shape))
    return pl.pallas_call(
        _rwkv_prep_kernel,
        grid=(m // tr,),
        in_specs=[pl.BlockSpec((tr, GW), lambda i: (r0 + i, 1)),
                  pl.BlockSpec((tr, R_SMALL), lambda i: (r0 + i, nsm)),
                  pspec((2, GW)), pspec((2, LANES, GW)), pspec((2, GW)), pspec((2, LANES, GW)),
                  pspec((2 * LANES, GW)), pspec((1, GW)), pspec((1, GW))],
        out_specs=[ospec] * 8,
        out_shape=[out] * 8,
        compiler_params=_cp("arbitrary"),
        name="rwkv_prep",
    )(y, y, prm["w0"], prm["w2p"], prm["a0"], prm["a2p"], prm["g2p"], prm["k_k"], prm["k_a"])


def _rwkv_scan_kernel(kk_f, r_f, v_f, w_f, b_f, kd_f, kk_b, r_b, v_b, w_b, b_b, kd_b,
                      s0f_ref, s0b_ref, of_ref, ob_ref, sf_ref, sb_ref, st_ref, red_ref, oacc_ref,
                      *, tc, gpc):
    c = pl.program_id(1)

    @pl.when(c == 0)
    def _():
        st_ref[0] = s0f_ref[...]
        st_ref[1] = s0b_ref[...]

    sub = 8
    ngrp = GW // LANES
    cr = gpc * HD_A
    nch = ngrp // gpc
    ones_bd = _ones_bd()
    rowi = lax.broadcasted_iota(jnp.int32, (cr, LANES), 0) % HD_A
    lane = lax.broadcasted_iota(jnp.int32, (cr, LANES), 1) % HD_A
    eye = jnp.where(rowi == lane, 1.0, 0.0).astype(BF16)
    dirs = ((kk_f, r_f, v_f, w_f, b_f, kd_f), (kk_b, r_b, v_b, w_b, b_b, kd_b))

    def bc(ref, base, rj, ch):
        r = ref[pl.ds(base, sub), pl.ds(ch * gpc * LANES, gpc * LANES)][rj:rj + 1, :]
        parts = [jnp.broadcast_to(r[:, q * LANES:(q + 1) * LANES], (HD_A, LANES)) for q in range(gpc)]
        return parts[0] if gpc == 1 else jnp.concatenate(parts, axis=0)

    def here(d, i, j):
        if d == 0:
            return pl.multiple_of(i * sub, sub), j
        return pl.multiple_of(tc - sub - i * sub, sub), sub - 1 - j

    def ahead(d, i, j):
        if j + 1 < sub:
            return here(d, i, j + 1)
        if d == 0:
            return pl.multiple_of(jnp.minimum(i * sub + sub, tc - sub), sub), 0
        return pl.multiple_of(jnp.maximum(tc - 2 * sub - i * sub, 0), sub), sub - 1

    for d, (kk_r, r_r, v_r, w_r, b_r, kd_r) in enumerate(dirs):
        base, rj = here(d, 0, 0)
        for ch in range(nch):
            rws = pl.ds(ch * cr, cr)
            sh = st_ref[d, rws, :].astype(BF16)
            lhs = jnp.concatenate([sh * bc(kk_r, base, rj, ch).astype(BF16),
                                   eye * bc(v_r, base, rj, ch).astype(BF16)], axis=0)
            red_ref[d, ch] = _dot(lhs, ones_bd)

    def step(i, carry):
        red = {(d, ch): red_ref[d, ch] for d in range(2) for ch in range(nch)}
        for j in range(sub):
            for d, (kk_r, r_r, v_r, w_r, b_r, kd_r) in enumerate(dirs):
                base, rj = here(d, i, j)
                nbase, nrj = ahead(d, i, j)
                for ch in range(nch):
                    rws = pl.ds(ch * cr, cr)
                    rd = red[(d, ch)]
                    s = (st_ref[d, rws, :] * bc(w_r, base, rj, ch) - rd[:cr] * bc(b_r, base, rj, ch)
                         + rd[cr:] * bc(kd_r, base, rj, ch))
                    st_ref[d, rws, :] = s
                    sh = s.astype(BF16)
                    lhs = jnp.concatenate([sh * bc(kk_r, nbase, nrj, ch).astype(BF16),
                                           eye * bc(v_r, nbase, nrj, ch).astype(BF16),
                                           sh * bc(r_r, base, rj, ch).astype(BF16)], axis=0)
                    out = _dot(lhs, ones_bd)
                    red[(d, ch)] = out[:2 * cr]
                    pltpu.store(oacc_ref.at[d, rws, :], out[2 * cr:], mask=lane == base + rj)
        for key, val in red.items():
            red_ref[key[0], key[1]] = val
        return carry

    lax.fori_loop(0, tc // sub, step, 0)

    for d, o_ref in enumerate((of_ref, ob_ref)):
        ot = oacc_ref[d].T
        top, bot = ot[:HD_A], ot[HD_A:]
        pieces = []
        for g in range(ngrp):
            pieces += [top[:, g * HD_A:(g + 1) * HD_A], bot[:, g * HD_A:(g + 1) * HD_A]]
        o_ref[...] = jnp.concatenate(pieces, axis=1)

    @pl.when(c == pl.num_programs(1) - 1)
    def _():
        sf_ref[...] = st_ref[0]
        sb_ref[...] = st_ref[1]


def _rwkv_scan(y, row0, nb, t, pre, s0f, s0b, tc=64, gpc=1):
    kk, wf, wb, bf, bb, kdf, kdb = pre
    assert tc == HD_A
    nc = t // tc
    yb = row0 // tc
    rows = GW // LANES * HD_A
    fw = lambda col: pl.BlockSpec((tc, GW), lambda b, c: (yb + b * nc + c, col))
    bw = lambda col: pl.BlockSpec((tc, GW), lambda b, c: (yb + b * nc + nc - 1 - c, col))
    fwp = pl.BlockSpec((tc, GW), lambda b, c: (b * nc + c, 0))
    bwp = pl.BlockSpec((tc, GW), lambda b, c: (b * nc + nc - 1 - c, 0))
    sspec = pl.BlockSpec((None, rows, LANES), lambda b, c: (b, 0, 0))
    oshape = jax.ShapeDtypeStruct((nb * t, GW), F32)
    sshape = jax.ShapeDtypeStruct((nb, rows, LANES), F32)
    return pl.pallas_call(
        functools.partial(_rwkv_scan_kernel, tc=tc, gpc=gpc),
        grid=(nb, nc),
        in_specs=[fwp, fw(0), fw(2), fwp, fwp, fwp, bwp, bw(0), bw(2), bwp, bwp, bwp, sspec, sspec],
        out_specs=[fwp, bwp, sspec, sspec],
        out_shape=[oshape, oshape, sshape, sshape],
        scratch_shapes=[pltpu.VMEM((2, rows, LANES), F32),
                        pltpu.VMEM((2, rows // (gpc * HD_A), 2 * gpc * HD_A, LANES), F32),
                        pltpu.VMEM((2, rows, LANES), F32)],
        compiler_params=_cp("arbitrary", "arbitrary"),
        name="rwkv_scan",
    )(kk, y, y, wf, bf, kdf, kk, y, y, wb, bb, kdb, s0f, s0b)


def _rwkv_post_kernel(of_ref, ob_ref, r_ref, v_ref, kdf_ref, kdb_ref, g_ref, lnw_ref, lnb_ref, rk_ref, o_ref):
    ones_bd = _ones_bd()
    o = of_ref[...] + ob_ref[...]
    mu = _headsum(o, ones_bd) * (1.0 / HD_A)
    dlt = o - mu
    var = _headsum(dlt * dlt, ones_bd) * (1.0 / HD_A)
    on = dlt * lax.rsqrt(var + GN_EPS) * lnw_ref[...] + lnb_ref[...]
    bonus = _headsum(r_ref[...] * (kdf_ref[...] + kdb_ref[...]) * rk_ref[...], ones_bd) * v_ref[...]
    o_ref[...] = ((on + bonus) * g_ref[...]).astype(o_ref.dtype)


def _rwkv_post(of, ob, y, row0, m, kdf, kdb, g, prm, l, dst, tr=256):
    tr = min(tr, m)
    r0 = row0 // tr
    rs = pl.BlockSpec((tr, GW), lambda i: (i, 0))
    ps = pl.BlockSpec((None, 1, GW), lambda i: (l, 0, 0))
    return _into(
        dst, _rwkv_post_kernel,
        grid=(m // tr,),
        in_specs=[rs, rs, pl.BlockSpec((tr, GW), lambda i: (r0 + i, 0)),
                  pl.BlockSpec((tr, GW), lambda i: (r0 + i, 2)), rs, rs, rs, ps, ps, ps],
        out_specs=pl.BlockSpec((tr, GW), lambda i: (r0 + i, 0)),
        compiler_params=_cp("arbitrary"),
        name="rwkv_post",
    )(of, ob, y, y, kdf, kdb, g, prm["ln_w"], prm["ln_b"], prm["r_k"])


def _rwkv_mixer(y, row0, nb, t, prm, l, s0f, s0b, dst):
    m = nb * t
    kk, wf, wb, bf, bb, kdf, kdb, g = _rwkv_prep(y, row0, m, prm, l)
    of, ob, sf, sb = _rwkv_scan(y, row0, nb, t, (kk, wf, wb, bf, bb, kdf, kdb), s0f, s0b)
    ya = _rwkv_post(of, ob, y, row0, m, kdf, kdb, g, prm, l, dst)
    return ya, sf, sb


def _state_in(s):
    b, h, v, k = s.shape
    return s.reshape(b, h // 2, 2, v, k).transpose(0, 1, 3, 2, 4).reshape(b, h // 2 * v, 2 * k)


def _state_out(s):
    b = s.shape[0]
    hp = s.shape[1] // HD_A
    return s.reshape(b, hp, HD_A, 2, HD_A).transpose(0, 1, 3, 2, 4).reshape(b, 2 * hp, HD_A, HD_A)


def _rope_tables(t):
    pos = jnp.arange(t)
    row = (pos // GRID_W).astype(F32)
    col = (pos % GRID_W).astype(F32)
    lane = jnp.arange(LANES)
    dd = lane % DK_B
    half = DK_B // 2
    nf = half // 2
    part = dd // half
    ii = dd % half
    inv = ROPE_BASE ** (-jnp.arange(nf, dtype=F32) / nf)
    p = jnp.where(part[None, :] == 0, row[:, None], col[:, None])
    ang = p * inv[ii % nf][None, :]
    cos, sin = jnp.cos(ang), jnp.sin(ang)
    first = (ii < nf)[None, :]
    return cos, jnp.where(first, -sin, 0.0), jnp.where(first, 0.0, sin), nf


def _rope(x, cos, s1, s2, nf):
    return x * cos + pltpu.roll(x, LANES - nf, 1) * s1 + pltpu.roll(x, nf, 1) * s2


def _split_heads(q, width):
    lane = lax.broadcasted_iota(jnp.int32, q.shape, 1)
    return jnp.concatenate([jnp.where(lane < width, q, 0.0), jnp.where(lane >= width, q, 0.0)],
                           axis=0).astype(BF16)


def _lam(lam_ref, lam_init):
    lv = lam_ref[...]
    a = jnp.sum(lv[0:1] * lv[1:2], axis=-1, keepdims=True)
    b = jnp.sum(lv[2:3] * lv[3:4], axis=-1, keepdims=True)
    return jnp.exp(a) - jnp.exp(b) + lam_init


def _diff_ctx_kernel(q_ref, k_ref, v_ref, lam_ref, g_ref, o_ref, ko_ref, vo_ref, *, lam_init):
    t = q_ref.shape[0]
    k = k_ref[...]
    v = v_ref[...]
    ko_ref[...] = k
    vo_ref[...] = v
    lam = _lam(lam_ref, lam_init)
    s = _dot_nt(_split_heads(q_ref[...] * (DK_B ** -0.5), DK_B), k.astype(BF16))
    e = jnp.exp(s - jnp.max(s, axis=-1, keepdims=True))
    p = e * (1.0 / jnp.sum(e, axis=-1, keepdims=True))
    o = _dot((p[:t] - lam * p[t:]).astype(BF16), v.astype(BF16))
    ms = jnp.mean(o * o, axis=-1, keepdims=True)
    o_ref[...] = (o * lax.rsqrt(ms + EPS) * g_ref[...] * (1.0 - lam_init)).astype(o_ref.dtype)


def _diff_ctx(y, row0, nb, t, lam_all, g_all, l, lam_init, col0, dst, dst_k, dst_v):
    nh = GW // LANES
    rb = row0 // t
    cq, ck, cv = col0 // LANES, (col0 + GW) // LANES, (col0 + 2 * GW) // LANES
    ysp = lambda cb: pl.BlockSpec((t, LANES), lambda b, h: (rb + b, cb + h))
    cspec = pl.BlockSpec((None, None, None, t, LANES), lambda b, h: (b, l, h, 0, 0))
    return _into(
        [dst, dst_k, dst_v], functools.partial(_diff_ctx_kernel, lam_init=lam_init),
        grid=(nb, nh),
        in_specs=[ysp(cq), ysp(ck), ysp(cv),
                  pl.BlockSpec((None, 4, DK_B), lambda b, h: (l, 0, 0)),
                  pl.BlockSpec((None, 1, LANES), lambda b, h: (l, 0, 0))],
        out_specs=[pl.BlockSpec((t, LANES), lambda b, h: (rb + b, h)), cspec, cspec],
        compiler_params=_cp("arbitrary", "arbitrary"),
        name="diff_ctx",
    )(y, y, y, lam_all, g_all)


def _diff_lat_kernel(q_ref, k_ref, v_ref, ck_ref, cv_ref, cos_ref, s1_ref, s2_ref, lam_ref, g_ref,
                     o_ref, kb_ref, vb_ref, *, lam_init, tq, nf):
    t = q_ref.shape[0]
    cos, s1, s2 = cos_ref[...], s1_ref[...], s2_ref[...]
    kb_ref[...] = _rope(k_ref[...], cos, s1, s2, nf).astype(BF16)
    vb_ref[...] = v_ref[...].astype(BF16)
    ckb = ck_ref[...].astype(BF16)
    cvb = cv_ref[...].astype(BF16)
    lam = _lam(lam_ref, lam_init)
    gsc = g_ref[...] * (1.0 - lam_init)

    def blk(i, carry):
        rows = pl.ds(pl.multiple_of(i * tq, tq), tq)
        q = _rope(q_ref[rows, :] * (DK_B ** -0.5), cos_ref[rows, :], s1_ref[rows, :], s2_ref[rows, :], nf)
        q2 = _split_heads(q, DK_B)
        sc = _dot_nt(q2, ckb)
        sl = _dot_nt(q2, kb_ref[...])
        mx = jnp.maximum(jnp.max(sc, axis=-1, keepdims=True), jnp.max(sl, axis=-1, keepdims=True))
        ec = jnp.exp(sc - mx)
        el = jnp.exp(sl - mx)
        rz = 1.0 / (jnp.sum(ec, axis=-1, keepdims=True) + jnp.sum(el, axis=-1, keepdims=True))
        pc = ec * rz
        plat = el * rz
        o = (_dot((pc[:tq] - lam * pc[tq:]).astype(BF16), cvb)
             + _dot((plat[:tq] - lam * plat[tq:]).astype(BF16), vb_ref[...]))
        ms = jnp.mean(o * o, axis=-1, keepdims=True)
        o_ref[rows, :] = (o * lax.rsqrt(ms + EPS) * gsc).astype(o_ref.dtype)
        return carry

    lax.fori_loop(0, t // tq, blk, 0)


def _diff_lat(y, row0, nb, t, cache_k, cache_v, lam_all, g_all, l, lam_init, col0, dst, tq=256):
    nh = GW // LANES
    rb = row0 // t
    tq = min(tq, t)
    cq, ck, cv = col0 // LANES, (col0 + GW) // LANES, (col0 + 2 * GW) // LANES
    ysp = lambda cb: pl.BlockSpec((t, LANES), lambda b, h: (rb + b, cb + h))
    past = cache_k.shape[3]
    csp = pl.BlockSpec((None, None, None, past, LANES), lambda b, h: (b, l, h, 0, 0))
    cos, s1, s2, nf = _rope_tables(t)
    tsp = pl.BlockSpec((t, LANES), lambda b, h: (0, 0))
    return _into(
        dst, functools.partial(_diff_lat_kernel, lam_init=lam_init, tq=tq, nf=nf),
        grid=(nb, nh),
        in_specs=[ysp(cq), ysp(ck), ysp(cv), csp, csp, tsp, tsp, tsp,
                  pl.BlockSpec((None, 4, DK_B), lambda b, h: (l, 0, 0)),
                  pl.BlockSpec((None, 1, LANES), lambda b, h: (l, 0, 0))],
        out_specs=pl.BlockSpec((t, LANES), lambda b, h: (rb + b, h)),
        scratch_shapes=[pltpu.VMEM((t, LANES), BF16), pltpu.VMEM((t, LANES), BF16)],
        compiler_params=_cp("arbitrary", "arbitrary"),
        name="diff_lat",
    )(y, y, y, cache_k, cache_v, cos, s1, s2, lam_all, g_all)


def _na_ctx_kernel(q_ref, k_ref, v_ref, o_ref, ko_ref, vo_ref):
    t = q_ref.shape[0]
    lane = lax.broadcasted_iota(jnp.int32, q_ref.shape, 1)
    k = k_ref[...]
    v = v_ref[...]
    for h in range(2):
        ko_ref[h] = k[:, h * HD_C:(h + 1) * HD_C]
        vo_ref[h] = v[:, h * HD_C:(h + 1) * HD_C]
    s = _dot_nt(_split_heads(q_ref[...] * (HD_C ** -0.5), HD_C), k.astype(BF16))
    e = jnp.exp(s - jnp.max(s, axis=-1, keepdims=True))
    p = e * (1.0 / jnp.sum(e, axis=-1, keepdims=True))
    o2 = _dot(p.astype(BF16), v.astype(BF16))
    o_ref[...] = jnp.where(lane < HD_C, o2[:t], o2[t:]).astype(o_ref.dtype)


def _na_ctx(y, row0, nb, t, l, col0, dst, dst_k, dst_v):
    npair = GW // LANES
    rb = row0 // t
    cq, ck, cv = col0 // LANES, (col0 + GW) // LANES, (col0 + 2 * GW) // LANES
    ysp = lambda cb: pl.BlockSpec((t, LANES), lambda b, p: (rb + b, cb + p))
    cspec = pl.BlockSpec((None, None, 2, t, HD_C), lambda b, p: (b, l, p, 0, 0))
    return _into(
        [dst, dst_k, dst_v], _na_ctx_kernel,
        grid=(nb, npair),
        in_specs=[ysp(cq), ysp(ck), ysp(cv)],
        out_specs=[pl.BlockSpec((t, LANES), lambda b, p: (rb + b, p)), cspec, cspec],
        compiler_params=_cp("arbitrary", "arbitrary"),
        name="na_ctx",
    )(y, y, y)


def _na_bias_kernel(rpb_ref, o_ref, t_ref):
    ww = lax.broadcasted_iota(jnp.int32, (GRID_W, GRID_W), 0)
    jj = lax.broadcasted_iota(jnp.int32, (GRID_W, GRID_W), 1)
    c0 = jnp.clip(ww - WIN_C // 2, 0, GRID_W - WIN_C)
    allowed = (jj >= c0) & (jj < c0 + WIN_C)
    diff = jj - ww + (WIN_C - 1)
    for dr in range(2 * WIN_R - 1):
        acc = jnp.full((GRID_W, GRID_W), NEG, F32)
        for dc in range(2 * WIN_C - 1):
            acc = jnp.where(diff == dc, rpb_ref[dr, dc], acc)
        t_ref[dr] = jnp.where(allowed, acc, NEG)
    for pat in range(WIN_R):
        for i in range(WIN_R):
            o_ref[pat, :, i * GRID_W:(i + 1) * GRID_W] = t_ref[i - pat + WIN_R - 1]


def _na_bias_table(rpb):
    nl, nh, ndr, ndc = rpb.shape
    return pl.pallas_call(
        _na_bias_kernel,
        grid=(nl, nh),
        in_specs=[pl.BlockSpec((None, None, ndr, ndc), lambda l, h: (l, h, 0, 0), memory_space=pltpu.SMEM)],
        out_specs=pl.BlockSpec((None, None, WIN_R, GRID_W, WIN_R * GRID_W), lambda l, h: (l, h // 2, 0, h % 2, 0)),
        out_shape=jax.ShapeDtypeStruct((nl, nh // 2, WIN_R, 2 * GRID_W, WIN_R * GRID_W), F32),
        scratch_shapes=[pltpu.VMEM((ndr, GRID_W, GRID_W), F32)],
        compiler_params=_cp("arbitrary", "arbitrary"),
        name="na_bias",
    )(rpb)


def _na_lat_kernel(q_ref, k_ref, v_ref, ck_ref, cv_ref, bias_ref, o_ref, kb_ref, vb_ref):
    t = q_ref.shape[0]
    rows = t // GRID_W
    kr = min(WIN_R, rows)
    kb_ref[...] = k_ref[...].astype(BF16)
    vb_ref[...] = v_ref[...].astype(BF16)
    ckb = ck_ref[...].astype(BF16)
    cvb = cv_ref[...].astype(BF16)
    lane = lax.broadcasted_iota(jnp.int32, (GRID_W, LANES), 1)

    def row_step(r, carry):
        r0 = jnp.clip(r - kr // 2, 0, rows - kr)
        pat = r - r0
        qrows = pl.ds(pl.multiple_of(r * GRID_W, GRID_W), GRID_W)
        wrows = pl.ds(pl.multiple_of(r0 * GRID_W, GRID_W), kr * GRID_W)
        q2 = _split_heads(q_ref[qrows, :] * (HD_C ** -0.5), HD_C)
        sw = _dot_nt(q2, kb_ref[wrows, :]) + bias_ref[pat]
        sc = _dot_nt(q2, ckb)
        mx = jnp.maximum(jnp.max(sw, axis=-1, keepdims=True), jnp.max(sc, axis=-1, keepdims=True))
        ew = jnp.exp(sw - mx)
        ec = jnp.exp(sc - mx)
        rz = 1.0 / (jnp.sum(ew, axis=-1, keepdims=True) + jnp.sum(ec, axis=-1, keepdims=True))
        o2 = _dot((ew * rz).astype(BF16), vb_ref[wrows, :]) + _dot((ec * rz).astype(BF16), cvb)
        o_ref[qrows, :] = jnp.where(lane < HD_C, o2[:GRID_W], o2[GRID_W:]).astype(o_ref.dtype)
        return carry

    lax.fori_loop(0, rows, row_step, 0, unroll=2)


def _na_lat(y, row0, nb, t, ctx_k, ctx_v, bias, l, col0, dst):
    npair = GW // LANES
    rb = row0 // t
    cq, ck, cv = col0 // LANES, (col0 + GW) // LANES, (col0 + 2 * GW) // LANES
    ysp = lambda cb: pl.BlockSpec((t, LANES), lambda b, p: (rb + b, cb + p))
    past = ctx_k.shape[3]
    csp = pl.BlockSpec((None, None, None, past, LANES), lambda b, p: (b, l, p, 0, 0))
    bsp = pl.BlockSpec((None, None, WIN_R, 2 * GRID_W, WIN_R * GRID_W), lambda b, p: (l, p, 0, 0, 0))
    return _into(
        dst, _na_lat_kernel,
        grid=(nb, npair),
        in_specs=[ysp(cq), ysp(ck), ysp(cv), csp, csp, bsp],
        out_specs=pl.BlockSpec((t, LANES), lambda b, p: (rb + b, p)),
        scratch_shapes=[pltpu.VMEM((t, LANES), BF16), pltpu.VMEM((t, LANES), BF16)],
        compiler_params=_cp("arbitrary", "arbitrary"),
        name="na_lat",
    )(y, y, y, ctx_k, ctx_v, bias)


def _hy_positions(length):
    t = jnp.linspace(0.0, 1.0, length, dtype=F32)[:, None]
    ang = 2.0 * math.pi * jnp.arange(length, dtype=F32) / length
    bands = jnp.linspace(1e-4, HY_BANDS - 1, HY_BANDS, dtype=F32)
    z = jnp.concatenate([t, jnp.cos(ang[:, None] * bands[None, :]),
                         -jnp.sin(ang[:, None] * bands[None, :])], axis=-1)
    z = jnp.pad(z, ((0, 0), (0, LANES - z.shape[1])))
    mn = math.log(HY_TARGET) / HY_SLOW
    mx = math.log(HY_TARGET) / HY_FAST
    deltas = jnp.abs(jnp.linspace(mn, mx, GW, dtype=F32))
    decay = jnp.exp(-t * deltas[None, :])
    return z, decay


def _hy_filter_kernel(z_ref, dec_ref, w1_ref, b1_ref, f1_ref, w2_ref, b2_ref, f2_ref, w3_ref, o_ref):
    hid = jnp.sin(f1_ref[...] * (_dot3(z_ref[...], w1_ref[...]) + b1_ref[...]))
    hid = jnp.sin(f2_ref[...] * (_dot3(hid, w2_ref[...]) + b2_ref[...]))
    h = _dot3(hid, w3_ref[...])
    dec = dec_ref[...]
    first = (lax.broadcasted_iota(jnp.int32, dec.shape, 0) + pl.program_id(0) * dec.shape[0]) == 0
    for o in range(h.shape[1] // (2 * GW)):
        hf = h[:, (2 * o) * GW:(2 * o + 1) * GW] * dec
        hb = jnp.where(first, 0.0, h[:, (2 * o + 1) * GW:(2 * o + 2) * GW] * dec)
        o_ref[:, (2 * o) * GW:(2 * o + 1) * GW] = hf + hb
        o_ref[:, (2 * o + 1) * GW:(2 * o + 2) * GW] = hf - hb


def _hy_filters(length, prm, l):
    z, decay = _hy_positions(length)
    n3 = prm["w3"].shape[2]
    hid = prm["w2"].shape[1]
    tl = min(length, 256)
    ps = lambda shape: pl.BlockSpec((None,) + shape, lambda i: (l,) + (0,) * len(shape))
    return pl.pallas_call(
        _hy_filter_kernel,
        grid=(length // tl,),
        in_specs=[pl.BlockSpec((tl, LANES), lambda i: (i, 0)), pl.BlockSpec((tl, GW), lambda i: (i, 0)),
                  ps((LANES, hid)), ps((1, hid)), ps((1, hid)), ps((hid, hid)), ps((1, hid)), ps((1, hid)),
                  ps((hid, n3))],
        out_specs=pl.BlockSpec((tl, n3), lambda i: (i, 0)),
        out_shape=jax.ShapeDtypeStruct((length, n3), F32),
        compiler_params=_cp("arbitrary"),
        name="hy_filter",
    )(z, decay, prm["w1p"], prm["b1"], prm["f1"], prm["w2"], prm["b2"], prm["f2"], prm["w3"])


def _dft_mats(length):
    n2 = 2 * length
    k = jnp.arange(length, dtype=jnp.int32)
    kn = (k[:, None] * k[None, :]) % n2
    ang = kn.astype(F32) * (2.0 * math.pi / n2)
    c, s = jnp.cos(ang), jnp.sin(ang)
    alt = jnp.where(k % 2 == 0, 1.0, -1.0).astype(F32)
    f_im = jnp.where((k == 0)[:, None], alt[None, :], -s)
    fwd = jnp.concatenate([c, f_im], axis=0).astype(BF16)
    ck = jnp.where(k == 0, 1.0, 2.0)[None, :]
    g_re = c * ck / n2
    g_im = jnp.where((k == 0)[None, :], alt[:, None] / n2, -2.0 * s / n2)
    inv = jnp.concatenate([g_re, g_im], axis=1).astype(BF16)
    return fwd, inv


def _mm_plain_kernel(a_ref, w_ref, o_ref):
    o_ref[...] = _dot(a_ref[...], w_ref[...].astype(BF16))


def _hy_spectrum(fwd, hk, tm=512, tn=512):
    m, k = fwd.shape
    n = hk.shape[1]
    tm, tn = min(tm, m), min(tn, n)
    return pl.pallas_call(
        _mm_plain_kernel,
        grid=(m // tm, n // tn),
        in_specs=[pl.BlockSpec((tm, k), lambda i, j: (i, 0)), pl.BlockSpec((k, tn), lambda i, j: (0, j))],
        out_specs=pl.BlockSpec((tm, tn), lambda i, j: (i, j)),
        out_shape=jax.ShapeDtypeStruct((m, n), F32),
        compiler_params=_cp("arbitrary", "arbitrary"),
        name="hy_spectrum",
    )(fwd, hk)


def _short_conv_kernel(u_ref, w_ref, b_ref, o_ref):
    u = u_ref[...]
    t = u.shape[0]
    rowi = lax.broadcasted_iota(jnp.int32, u.shape, 0)
    prev = jnp.where(rowi == 0, 0.0, pltpu.roll(u, 1, 0))
    nxt = jnp.where(rowi == t - 1, 0.0, pltpu.roll(u, t - 1, 0))
    o_ref[...] = prev * w_ref[0:1, :] + u * w_ref[1:2, :] + nxt * w_ref[2:3, :] + b_ref[...]


def _short_conv(y, row0, nb, t, w_all, b_all, l, col0, width, tc=512):
    rb = row0 // t
    cb = col0 // tc
    return pl.pallas_call(
        _short_conv_kernel,
        grid=(nb, width // tc),
        in_specs=[pl.BlockSpec((t, tc), lambda b, j: (rb + b, cb + j)),
                  pl.BlockSpec((None, 3, tc), lambda b, j: (l, 0, j)),
                  pl.BlockSpec((None, 1, tc), lambda b, j: (l, 0, j))],
        out_specs=pl.BlockSpec((t, tc), lambda b, j: (b, j)),
        out_shape=jax.ShapeDtypeStruct((nb * t, width), F32),
        compiler_params=_cp("arbitrary", "arbitrary"),
        name="short_conv",
    )(y, w_all, b_all)


def _hy_fwd_kernel(fre_ref, fim_ref, z_ref, kre_ref, kny_ref, kim_ref, o_ref):
    zb = z_ref[...].astype(BF16)
    re = _dot(fre_ref[...], zb)
    im = _dot(fim_ref[...], zb)
    tm = re.shape[0]
    dc = (lax.broadcasted_iota(jnp.int32, re.shape, 0) + pl.program_id(1) * tm) == 0
    kre = kre_ref[...]
    kim = jnp.where(dc, kny_ref[...], kim_ref[...])
    o_ref[0] = jnp.where(dc, re * kre, re * kre - im * kim).astype(o_ref.dtype)
    o_ref[1] = jnp.where(dc, im * kim, re * kim + im * kre).astype(o_ref.dtype)


def _hy_fwd(fwd, z, zcol, nb, t, kspec, o, tm=512, tn=512):
    tm, tn = min(tm, t), min(tn, GW)
    ni = t // tm
    zc = zcol // tn
    cs, cd = (2 * o) * GW // tn, (2 * o + 1) * GW // tn
    return pl.pallas_call(
        _hy_fwd_kernel,
        grid=(nb, ni, GW // tn),
        in_specs=[pl.BlockSpec((tm, t), lambda b, i, j: (i, 0)),
                  pl.BlockSpec((tm, t), lambda b, i, j: (ni + i, 0)),
                  pl.BlockSpec((t, tn), lambda b, i, j: (b, zc + j)),
                  pl.BlockSpec((tm, tn), lambda b, i, j: (i, cs + j)),
                  pl.BlockSpec((tm, tn), lambda b, i, j: (ni + i, cs + j)),
                  pl.BlockSpec((tm, tn), lambda b, i, j: (ni + i, cd + j))],
        out_specs=pl.BlockSpec((None, 2, tm, tn), lambda b, i, j: (b, 0, i, j)),
        out_shape=jax.ShapeDtypeStruct((nb, 2, t, GW), BF16),
        compiler_params=_cp("arbitrary", "arbitrary", "arbitrary"),
        name="hy_fwd",
    )(fwd, fwd, z, kspec, kspec, kspec)


def _hy_inv_kernel(g_ref, s_ref, zin_ref, gate_ref, skip_ref, o_ref):
    yv = _dot(g_ref[...], s_ref[...])
    o_ref[...] = (gate_ref[...] * (yv + skip_ref[...] * zin_ref[...])).astype(o_ref.dtype)


def _hy_inv(inv, spec, z, zcol, gate, gcol, skip_all, l, o, nb, t, dst, row0=0, tm=512, tn=512):
    tm, tn = min(tm, t), min(tn, GW)
    ni = t // tm
    zc, gc = zcol // tn, gcol // tn
    ro = row0 // tm
    return _into(
        dst, _hy_inv_kernel,
        grid=(nb, ni, GW // tn),
        in_specs=[pl.BlockSpec((tm, 2 * t), lambda b, i, j: (i, 0)),
                  pl.BlockSpec((None, 2 * t, tn), lambda b, i, j: (b, 0, j)),
                  pl.BlockSpec((tm, tn), lambda b, i, j: (b * ni + i, zc + j)),
                  pl.BlockSpec((tm, tn), lambda b, i, j: (b * ni + i, gc + j)),
                  pl.BlockSpec((None, None, 1, tn), lambda b, i, j: (l, o, 0, j))],
        out_specs=pl.BlockSpec((tm, tn), lambda b, i, j: (ro + b * ni + i, j)),
        compiler_params=_cp("arbitrary", "arbitrary", "arbitrary"),
        name="hy_inv",
    )(inv, spec.reshape(nb, 2 * t, GW), z, gate, skip_all)


def _hyena_mixer(y, row0, nb, t, prm, l, col0, consts, dst):
    fwd, inv, kspec = consts
    u = _short_conv(y, row0, nb, t, prm["sconv_w"], prm["sconv_b"], l, col0, 3 * GW)
    spec = _hy_fwd(fwd, u, 0, nb, t, kspec, 0)
    z = _hy_inv(inv, spec, u, 0, u, GW, prm["skip"], l, 0, nb, t, jax.ShapeDtypeStruct((nb * t, GW), F32))
    spec = _hy_fwd(fwd, z, 0, nb, t, kspec, 1)
    return _hy_inv(inv, spec, z, 0, u, 2 * GW, prm["skip"], l, 1, nb, t, dst, row0)


def _hyena_consts(t, prm, l):
    fwd, inv = _dft_mats(t)
    return fwd, inv, _hy_spectrum(fwd, _hy_filters(t, prm, l))


def _permute_w_in(w_in):
    gw = GW
    big = w_in[..., :3 * gw]
    o = 3 * gw
    small = w_in[..., o:o + 416]
    rest = w_in[..., o + 416:]
    pad = jnp.zeros(w_in.shape[:-1] + (R_SMALL - 416,), w_in.dtype)
    return jnp.concatenate([big, rest, small, pad], axis=-1).astype(BF16)


def _pad_rows(w, rows, at=0):
    pad = [(0, 0)] * (w.ndim - 2) + [(at, rows - at - w.shape[-2]), (0, 0)]
    return jnp.pad(w, pad)


def kernel(x_prompt, x_sample, cache_diff_k, cache_diff_v, cache_na_k, cache_na_v, state_rwkv_fwd,
           state_rwkv_bwd, c, c_ctx, norm1_g, norm2_g, w_mod, b_mod, w_in, rwkv_w0, rwkv_w2, rwkv_a0,
           rwkv_a2, rwkv_g2, rwkv_k_k, rwkv_k_a, rwkv_r_k, rwkv_ln_w, rwkv_ln_b, diff_lam,
           diff_subln_g, na_rpb, hy_sconv_w, hy_sconv_b, hy_f_w1, hy_f_b1, hy_f_freq1, hy_f_w2,
           hy_f_b2, hy_f_freq2, hy_f_w3, hy_skip, w_out, w_up, w_down, final_g):
    bp, tp, d = x_prompt.shape
    bs, ts, _ = x_sample.shape
    nl = w_in.shape[0]
    mp, ms = bp * tp, bs * ts
    assert mp % 1024 == 0 and ts % 1024 == 0 and bs + 1 <= 8
    grp = _group_fn(mp, ts)

    mt = mp + ms
    xp, xs = x_prompt.reshape(mp, d), x_sample.reshape(ms, d)
    cv = jnp.concatenate([c_ctx[None], c, jnp.zeros((8 - 1 - bs, d), F32)], axis=0)
    mod = _modulation(cv, w_mod, b_mod).reshape(nl, 8, N_MOD, 1, d)

    w_in_b = _permute_w_in(w_in)
    w_out_b = w_out.astype(BF16)
    w_up_b = w_up.astype(BF16)
    w_down_b = w_down.astype(BF16)
    row = lambda p: p.reshape(nl, 1, -1)
    rw = {
        "w0": rwkv_w0, "a0": rwkv_a0,
        "w2p": jnp.stack([_pad_rows(rwkv_w2[:, 0], LANES, 0), _pad_rows(rwkv_w2[:, 1], LANES, 64)], axis=1),
        "a2p": jnp.stack([_pad_rows(rwkv_a2[:, 0], LANES, 0), _pad_rows(rwkv_a2[:, 1], LANES, 64)], axis=1),
        "g2p": _pad_rows(rwkv_g2, 2 * LANES, 0),
        "k_k": row(rwkv_k_k), "k_a": row(rwkv_k_a), "r_k": row(rwkv_r_k),
        "ln_w": row(rwkv_ln_w), "ln_b": row(rwkv_ln_b),
    }
    hy = {
        "sconv_w": hy_sconv_w, "sconv_b": hy_sconv_b.reshape(nl, 1, -1),
        "w1p": _pad_rows(hy_f_w1, LANES, 0), "b1": row(hy_f_b1), "f1": row(hy_f_freq1),
        "w2": hy_f_w2, "b2": row(hy_f_b2), "f2": row(hy_f_freq2), "w3": hy_f_w3,
        "skip": hy_skip.reshape(nl, 2, 1, GW),
    }
    subln = diff_subln_g.reshape(nl, 1, -1)
    norm1 = norm1_g.reshape(nl, 1, d)
    norm2 = norm2_g.reshape(nl, 1, d)
    na_bias = _na_bias_table(na_rpb)
    pair = lambda a: a.reshape(a.shape[0], a.shape[1], a.shape[2] // 2, 2, a.shape[3], a.shape[4]) \
        .transpose(0, 1, 2, 4, 3, 5).reshape(a.shape[0], a.shape[1], a.shape[2] // 2, a.shape[3], 2 * a.shape[4])
    na_ck, na_cv = pair(cache_na_k), pair(cache_na_v)
    zero_state = jnp.zeros((bp, GW // LANES * HD_A, LANES), F32)

    c_b, c_c, c_d = 3 * GW, 6 * GW, 9 * GW
    act = lambda dtype, width=d: jax.ShapeDtypeStruct((mt, width), dtype)
    xsrc = ((xp, 0, 0, mp), (xs, 0, mp, ms))
    dk = jax.ShapeDtypeStruct((bp, nl, GW // LANES, tp, LANES), F32)
    nk = jax.ShapeDtypeStruct((bp, nl, GW // HD_C, tp, HD_C), F32)
    dv, nv = dk, nk
    states = [[], []]
    for l in range(nl):
        lam_init = 0.8 - 0.6 * math.exp(-0.3 * l)
        h = act(BF16)
        for xa, xrow0, orow0, m in xsrc:
            h = _norm_mod(xa, xrow0, m, norm1, mod, l, 1, 0, grp, h, orow0)
        y = _mm_fullk([h], w_in_b, l, act(F32, w_in_b.shape[2]), 0, mt, name="proj_in")

        ya, sf, sb = _rwkv_mixer(y, 0, bp, tp, rw, l, zero_state, zero_state, act(BF16, GW))
        yb, dk, dv = _diff_ctx(y, 0, bp, tp, diff_lam, subln, l, lam_init, c_b, act(BF16, GW), dk, dv)
        yc, nk, nv = _na_ctx(y, 0, bp, tp, l, c_c, act(BF16, GW), nk, nv)
        yd = _hyena_mixer(y, 0, bp, tp, hy, l, c_d, _hyena_consts(tp, hy, l), act(BF16, GW))
        ya, _, _ = _rwkv_mixer(y, mp, bs, ts, rw, l, _state_in(state_rwkv_fwd[:, l]),
                               _state_in(state_rwkv_bwd[:, l]), ya)
        yb = _diff_lat(y, mp, bs, ts, cache_diff_k, cache_diff_v, diff_lam, subln, l, lam_init, c_b, yb)
        yc = _na_lat(y, mp, bs, ts, na_ck, na_cv, na_bias, l, c_c, yc)
        yd = _hyena_mixer(y, mp, bs, ts, hy, l, c_d, _hyena_consts(ts, hy, l), yd)

        x = act(F32)
        for xa, xrow0, orow0, m in xsrc:
            x = _mm_fullk([ya, yb, yc, yd], w_out_b, l, x, orow0, m, epilogue="resid",
                          resid=(xa, xrow0, mod, 2, grp), name="proj_out")
        h = _norm_mod(x, 0, mt, norm2, mod, l, 4, 3, grp, act(BF16), 0)
        u = _mm_fullk([h], w_up_b, l, act(BF16, w_up_b.shape[2]), 0, mt, epilogue="relu2", name="ffn_up")
        x = _mm_kloop_resid(u, w_down_b, l, x, mod, 5, grp)
        xsrc = ((x, 0, 0, mt),)
        states[0].append(_state_out(sf))
        states[1].append(_state_out(sb))

    return (_rmsnorm(x, 0, mp, final_g).reshape(bp, tp, d), _rmsnorm(x, mp, ms, final_g).reshape(bs, ts, d),
            dk, dv, nk, nv, jnp.stack(states[0], axis=1), jnp.stack(states[1], axis=1))
```

```python
import functools
import math

import jax
import jax.numpy as jnp
import numpy as np
from jax import lax
from jax.experimental import pallas as pl
from jax.experimental.pallas import tpu as pltpu

F32 = jnp.float32
BF16 = jnp.bfloat16

GRID_W = 64
HD_A = 64
DK_B = 64
HD_C = 64
WIN_R = 8
WIN_C = 16
GW = 1024
R_SMALL = 512
N_MOD = 6
GN_EPS = 64e-5
EPS = 1e-6
ROPE_BASE = 10000.0
HY_BANDS = 16
HY_TARGET = 1e-2
HY_FAST = 0.3
HY_SLOW = 1.5
NEG = -1e30
LANES = 128
VMEM_LIMIT = 56 * 1024 * 1024


def _cp(*sem):
    return pltpu.CompilerParams(dimension_semantics=sem, vmem_limit_bytes=VMEM_LIMIT)


def _into(dst, kernel, *, in_specs, out_specs, **kw):
    multi = isinstance(out_specs, (list, tuple))
    dsts = list(dst) if multi else [dst]
    shapes = [d if isinstance(d, jax.ShapeDtypeStruct) else jax.ShapeDtypeStruct(d.shape, d.dtype) for d in dsts]
    held = [(o, d) for o, d in enumerate(dsts) if not isinstance(d, jax.ShapeDtypeStruct)]
    n_in = len(in_specs)

    def body(*refs):
        kernel(*refs[:n_in], *refs[n_in + len(held):])

    call = pl.pallas_call(body, in_specs=[*in_specs] + [pl.BlockSpec(memory_space=pl.ANY)] * len(held),
                          out_specs=out_specs, out_shape=shapes if multi else shapes[0],
                          input_output_aliases={n_in + i: o for i, (o, _) in enumerate(held)}, **kw)
    return lambda *args: call(*args, *[d for _, d in held])


def _split_bf16(x):
    hi = x.astype(BF16)
    lo = (x - hi.astype(F32)).astype(BF16)
    return hi, lo


def _dot(a, b):
    return jnp.dot(a, b, preferred_element_type=F32)


def _dot_nt(a, b):
    return lax.dot_general(a, b, (((1,), (1,)), ((), ())), preferred_element_type=F32)


def _dot3(a, b):
    ah, al = _split_bf16(a)
    bh, bl = _split_bf16(b)
    return _dot(ah, bh) + _dot(al, bh) + _dot(ah, bl)


def _ones_bd(n=LANES, blk=HD_A):
    r = lax.broadcasted_iota(jnp.int32, (n, n), 0) // blk
    c = lax.broadcasted_iota(jnp.int32, (n, n), 1) // blk
    return jnp.where(r == c, 1.0, 0.0).astype(BF16)


def _headsum(x, ones_bd):
    hi, lo = _split_bf16(x)
    outs = []
    for g in range(x.shape[1] // LANES):
        sl = slice(g * LANES, (g + 1) * LANES)
        outs.append(_dot(hi[:, sl], ones_bd) + _dot(lo[:, sl], ones_bd))
    return jnp.concatenate(outs, axis=1) if len(outs) > 1 else outs[0]


def _group_fn(mp, ts):
    def g(row0):
        return jnp.where(row0 < mp, 0, 1 + (row0 - mp) // ts)
    return g


def _mod_kernel(c_ref, w_ref, b_ref, o_ref):
    c = c_ref[...]
    s = c * jax.nn.sigmoid(c)
    o_ref[...] = _dot3(s, w_ref[...]) + b_ref[...]


def _modulation(cv, w_mod, b_mod):
    nl, d, n = w_mod.shape
    tn = 512
    return pl.pallas_call(
        _mod_kernel,
        grid=(nl, n // tn),
        in_specs=[pl.BlockSpec((8, d), lambda l, j: (0, 0)),
                  pl.BlockSpec((None, d, tn), lambda l, j: (l, 0, j)),
                  pl.BlockSpec((None, 1, tn), lambda l, j: (l, 0, j))],
        out_specs=pl.BlockSpec((None, 8, tn), lambda l, j: (l, 0, j)),
        out_shape=jax.ShapeDtypeStruct((nl, 8, n), F32),
        compiler_params=_cp("arbitrary", "arbitrary"),
        name="modulation",
    )(cv, w_mod, b_mod.reshape(nl, 1, n))


def _norm_mod_kernel(x_ref, g_ref, sc_ref, sh_ref, o_ref):
    x = x_ref[...]
    ms = jnp.mean(x * x, axis=-1, keepdims=True)
    y = x * lax.rsqrt(ms + EPS) * g_ref[...]
    o_ref[...] = (y * (1.0 + sc_ref[...]) + sh_ref[...]).astype(o_ref.dtype)


def _norm_mod(x, xrow0, m, g_all, mod, l, which_sc, which_sh, grp, dst, orow0, tr=256):
    d = x.shape[1]
    tr = min(tr, m)
    xr, orr = xrow0 // tr, orow0 // tr
    mspec = lambda w: pl.BlockSpec((None, None, None, 1, d), lambda i: (l, grp(orow0 + i * tr), w, 0, 0))
    return _into(
        dst, _norm_mod_kernel,
        grid=(m // tr,),
        in_specs=[pl.BlockSpec((tr, d), lambda i: (xr + i, 0)),
                  pl.BlockSpec((None, 1, d), lambda i: (l, 0, 0)),
                  mspec(which_sc), mspec(which_sh)],
        out_specs=pl.BlockSpec((tr, d), lambda i: (orr + i, 0)),
        compiler_params=_cp("arbitrary"),
        name="norm_mod",
    )(x, g_all, mod, mod)


def _rmsnorm_kernel(x_ref, g_ref, o_ref):
    x = x_ref[...]
    ms = jnp.mean(x * x, axis=-1, keepdims=True)
    o_ref[...] = x * lax.rsqrt(ms + EPS) * g_ref[...]


def _rmsnorm(x, xrow0, m, g, tr=256):
    d = x.shape[1]
    tr = min(tr, m)
    xr = xrow0 // tr
    return pl.pallas_call(
        _rmsnorm_kernel,
        grid=(m // tr,),
        in_specs=[pl.BlockSpec((tr, d), lambda i: (xr + i, 0)),
                  pl.BlockSpec((1, d), lambda i: (0, 0))],
        out_specs=pl.BlockSpec((tr, d), lambda i: (i, 0)),
        out_shape=jax.ShapeDtypeStruct((m, d), F32),
        compiler_params=_cp("arbitrary"),
        name="final_norm",
    )(x, g.reshape(1, d))


def _mm_fullk_kernel(*refs, n_a, epilogue):
    a_refs = refs[:n_a]
    w_ref = refs[n_a]
    o_ref = refs[-1]
    acc = None
    k0 = 0
    for a_ref in a_refs:
        kw = a_ref.shape[1]
        part = _dot(a_ref[...], w_ref[k0:k0 + kw, :])
        acc = part if acc is None else acc + part
        k0 += kw
    if epilogue == "relu2":
        acc = jnp.square(jnp.maximum(acc, 0.0))
    elif epilogue == "resid":
        x_ref, gate_ref = refs[n_a + 1], refs[n_a + 2]
        acc = x_ref[...] + gate_ref[...] * acc
    o_ref[...] = acc.astype(o_ref.dtype)


def _mm_fullk(a_list, w_all, l, dst, row0, m, epilogue="none", resid=None, tm=1024, tn=512, name="mm"):
    k, n = w_all.shape[1], w_all.shape[2]
    tm, tn = min(tm, m), min(tn, n)
    rb = row0 // tm
    in_specs = [pl.BlockSpec((tm, a.shape[1]), lambda i, j: (rb + i, 0)) for a in a_list]
    in_specs.append(pl.BlockSpec((None, k, tn), lambda i, j: (l, 0, j)))
    args = list(a_list) + [w_all]
    if epilogue == "resid":
        x, xrow0, mod, which, grp = resid
        xb = xrow0 // tm
        in_specs.append(pl.BlockSpec((tm, tn), lambda i, j: (xb + i, j)))
        in_specs.append(pl.BlockSpec((None, None, None, 1, tn),
                                     lambda i, j: (l, grp(row0 + i * tm), which, 0, j)))
        args += [x, mod]
    return _into(
        dst, functools.partial(_mm_fullk_kernel, n_a=len(a_list), epilogue=epilogue),
        grid=(m // tm, n // tn),
        in_specs=in_specs,
        out_specs=pl.BlockSpec((tm, tn), lambda i, j: (rb + i, j)),
        compiler_params=_cp("arbitrary", "arbitrary"),
        name=name,
    )(*args)


def _mm_kloop_kernel(a_ref, w_ref, x_ref, gate_ref, o_ref, acc_ref):
    kk = pl.program_id(2)

    @pl.when(kk == 0)
    def _():
        acc_ref[...] = jnp.zeros_like(acc_ref)

    acc_ref[...] += _dot(a_ref[...], w_ref[...])

    @pl.when(kk == pl.num_programs(2) - 1)
    def _():
        o_ref[...] = x_ref[...] + gate_ref[...] * acc_ref[...]


def _mm_kloop_resid(a, w_all, l, x, mod, which, grp, tm=1024, tn=1024, tk=2048):
    m, k = a.shape
    n = w_all.shape[2]
    tm, tn, tk = min(tm, m), min(tn, n), min(tk, k)
    return pl.pallas_call(
        _mm_kloop_kernel,
        grid=(m // tm, n // tn, k // tk),
        in_specs=[pl.BlockSpec((tm, tk), lambda i, j, q: (i, q)),
                  pl.BlockSpec((None, tk, tn), lambda i, j, q: (l, q, j)),
                  pl.BlockSpec((tm, tn), lambda i, j, q: (i, j)),
                  pl.BlockSpec((None, None, None, 1, tn),
                               lambda i, j, q: (l, grp(i * tm), which, 0, j))],
        out_specs=pl.BlockSpec((tm, tn), lambda i, j, q: (i, j)),
        out_shape=jax.ShapeDtypeStruct((m, n), F32),
        scratch_shapes=[pltpu.VMEM((tm, tn), F32)],
        compiler_params=_cp("arbitrary", "arbitrary", "arbitrary"),
        name="ffn_down",
    )(a, w_all, x, mod)


def _softplus(z):
    return jnp.maximum(z, 0.0) + jnp.log(1.0 + jnp.exp(-jnp.abs(z)))


def _rwkv_prep_kernel(k_ref, sm_ref, w0_ref, w2_ref, a0_ref, a2_ref, g2_ref, kk_ref_p, ka_ref_p,
                      kk_o, wf_o, wb_o, bf_o, bb_o, kdf_o, kdb_o, g_o):
    ones_bd = _ones_bd()
    k = k_ref[...]
    kraw = k * kk_ref_p[...]
    ss = _headsum(kraw * kraw, ones_bd)
    kk = kraw / jnp.maximum(jnp.sqrt(ss), 1e-12)
    kk_o[...] = kk
    sm = sm_ref[...]
    wd = jnp.tanh(sm[:, 0:LANES])
    ad = sm[:, LANES:2 * LANES]
    gd = jax.nn.sigmoid(sm[:, 2 * LANES:4 * LANES])
    ka = ka_ref_p[...]
    for d, (w_o, b_o, kd_o) in enumerate(((wf_o, bf_o, kdf_o), (wb_o, bb_o, kdb_o))):
        wl = w0_ref[d:d + 1, :] + _dot3(wd, w2_ref[d])
        wlog = -_softplus(-wl) - 0.5
        w_o[...] = jnp.exp(-jnp.exp(wlog))
        a = jax.nn.sigmoid(a0_ref[d:d + 1, :] + _dot3(ad, a2_ref[d]))
        b_o[...] = kk * a
        kd_o[...] = k * (1.0 + (a - 1.0) * ka)
    g_o[...] = _dot3(gd, g2_ref[...])


def _rwkv_prep(y, row0, m, prm, l, tr=256):
    tr = min(tr, m)
    r0 = row0 // tr
    nsm = y.shape[1] // R_SMALL - 1
    out = jax.ShapeDtypeStruct((m, GW), F32)
    ospec = pl.BlockSpec((tr, GW), lambda i: (i, 0))
    pspec = lambda shape: pl.BlockSpec((None,) + shape, lambda i: (l,) + (0,) * len(shape))
    return pl.pallas_call(
        _rwkv_prep_kernel,
        grid=(m // tr,),
        in_specs=[pl.BlockSpec((tr, GW), lambda i: (r0 + i, 1)),
                  pl.BlockSpec((tr, R_SMALL), lambda i: (r0 + i, nsm)),
                  pspec((2, GW)), pspec((2, LANES, GW)), pspec((2, GW)), pspec((2, LANES, GW)),
                  pspec((2 * LANES, GW)), pspec((1, GW)), pspec((1, GW))],
        out_specs=[ospec] * 8,
        out_shape=[out] * 8,
        compiler_params=_cp("arbitrary"),
        name="rwkv_prep",
    )(y, y, prm["w0"], prm["w2p"], prm["a0"], prm["a2p"], prm["g2p"], prm["k_k"], prm["k_a"])


def _rwkv_scan_kernel(kk_f, r_f, v_f, w_f, b_f, kd_f, kk_b, r_b, v_b, w_b, b_b, kd_b,
                      s0f_ref, s0b_ref, of_ref, ob_ref, sf_ref, sb_ref, st_ref, red_ref, oacc_ref,
                      *, tc, gpc):
    c = pl.program_id(1)

    @pl.when(c == 0)
    def _():
        st_ref[0] = s0f_ref[...]
        st_ref[1] = s0b_ref[...]

    sub = 8
    ngrp = GW // LANES
    cr = gpc * HD_A
    nch = ngrp // gpc
    ones_bd = _ones_bd()
    rowi = lax.broadcasted_iota(jnp.int32, (cr, LANES), 0) % HD_A
    lane = lax.broadcasted_iota(jnp.int32, (cr, LANES), 1) % HD_A
    eye = jnp.where(rowi == lane, 1.0, 0.0).astype(BF16)
    dirs = ((kk_f, r_f, v_f, w_f, b_f, kd_f), (kk_b, r_b, v_b, w_b, b_b, kd_b))

    def bc(ref, base, rj, ch):
        r = ref[pl.ds(base, sub), pl.ds(ch * gpc * LANES, gpc * LANES)][rj:rj + 1, :]
        parts = [jnp.broadcast_to(r[:, q * LANES:(q + 1) * LANES], (HD_A, LANES)) for q in range(gpc)]
        return parts[0] if gpc == 1 else jnp.concatenate(parts, axis=0)

    def here(d, i, j):
        if d == 0:
            return pl.multiple_of(i * sub, sub), j
        return pl.multiple_of(tc - sub - i * sub, sub), sub - 1 - j

    def ahead(d, i, j):
        if j + 1 < sub:
            return here(d, i, j + 1)
        if d == 0:
            return pl.multiple_of(jnp.minimum(i * sub + sub, tc - sub), sub), 0
        return pl.multiple_of(jnp.maximum(tc - 2 * sub - i * sub, 0), sub), sub - 1

    for d, (kk_r, r_r, v_r, w_r, b_r, kd_r) in enumerate(dirs):
        base, rj = here(d, 0, 0)
        for ch in range(nch):
            rws = pl.ds(ch * cr, cr)
            sh = st_ref[d, rws, :].astype(BF16)
            lhs = jnp.concatenate([sh * bc(kk_r, base, rj, ch).astype(BF16),
                                   eye * bc(v_r, base, rj, ch).astype(BF16)], axis=0)
            red_ref[d, ch] = _dot(lhs, ones_bd)

    def step(i, carry):
        red = {(d, ch): red_ref[d, ch] for d in range(2) for ch in range(nch)}
        for j in range(sub):
            for d, (kk_r, r_r, v_r, w_r, b_r, kd_r) in enumerate(dirs):
                base, rj = here(d, i, j)
                nbase, nrj = ahead(d, i, j)
                for ch in range(nch):
                    rws = pl.ds(ch * cr, cr)
                    rd = red[(d, ch)]
                    s = (st_ref[d, rws, :] * bc(w_r, base, rj, ch) - rd[:cr] * bc(b_r, base, rj, ch)
                         + rd[cr:] * bc(kd_r, base, rj, ch))
                    st_ref[d, rws, :] = s
                    sh = s.astype(BF16)
                    lhs = jnp.concatenate([sh * bc(kk_r, nbase, nrj, ch).astype(BF16),
                                           eye * bc(v_r, nbase, nrj, ch).astype(BF16),
                                           sh * bc(r_r, base, rj, ch).astype(BF16)], axis=0)
                    out = _dot(lhs, ones_bd)
                    red[(d, ch)] = out[:2 * cr]
                    pltpu.store(oacc_ref.at[d, rws, :], out[2 * cr:], mask=lane == base + rj)
        for key, val in red.items():
            red_ref[key[0], key[1]] = val
        return carry

    lax.fori_loop(0, tc // sub, step, 0)

    for d, o_ref in enumerate((of_ref, ob_ref)):
        ot = oacc_ref[d].T
        top, bot = ot[:HD_A], ot[HD_A:]
        pieces = []
        for g in range(ngrp):
            pieces += [top[:, g * HD_A:(g + 1) * HD_A], bot[:, g * HD_A:(g + 1) * HD_A]]
        o_ref[...] = jnp.concatenate(pieces, axis=1)

    @pl.when(c == pl.num_programs(1) - 1)
    def _():
        sf_ref[...] = st_ref[0]
        sb_ref[...] = st_ref[1]


def _rwkv_scan(y, row0, nb, t, pre, s0f, s0b, tc=64, gpc=1):
    kk, wf, wb, bf, bb, kdf, kdb = pre
    assert tc == HD_A
    nc = t // tc
    yb = row0 // tc
    rows = GW // LANES * HD_A
    fw = lambda col: pl.BlockSpec((tc, GW), lambda b, c: (yb + b * nc + c, col))
    bw = lambda col: pl.BlockSpec((tc, GW), lambda b, c: (yb + b * nc + nc - 1 - c, col))
    fwp = pl.BlockSpec((tc, GW), lambda b, c: (b * nc + c, 0))
    bwp = pl.BlockSpec((tc, GW), lambda b, c: (b * nc + nc - 1 - c, 0))
    sspec = pl.BlockSpec((None, rows, LANES), lambda b, c: (b, 0, 0))
    oshape = jax.ShapeDtypeStruct((nb * t, GW), F32)
    sshape = jax.ShapeDtypeStruct((nb, rows, LANES), F32)
    return pl.pallas_call(
        functools.partial(_rwkv_scan_kernel, tc=tc, gpc=gpc),
        grid=(nb, nc),
        in_specs=[fwp, fw(0), fw(2), fwp, fwp, fwp, bwp, bw(0), bw(2), bwp, bwp, bwp, sspec, sspec],
        out_specs=[fwp, bwp, sspec, sspec],
        out_shape=[oshape, oshape, sshape, sshape],
        scratch_shapes=[pltpu.VMEM((2, rows, LANES), F32),
                        pltpu.VMEM((2, rows // (gpc * HD_A), 2 * gpc * HD_A, LANES), F32),
                        pltpu.VMEM((2, rows, LANES), F32)],
        compiler_params=_cp("arbitrary", "arbitrary"),
        name="rwkv_scan",
    )(kk, y, y, wf, bf, kdf, kk, y, y, wb, bb, kdb, s0f, s0b)


def _rwkv_post_kernel(of_ref, ob_ref, r_ref, v_ref, kdf_ref, kdb_ref, g_ref, lnw_ref, lnb_ref, rk_ref, o_ref):
    ones_bd = _ones_bd()
    o = of_ref[...] + ob_ref[...]
    mu = _headsum(o, ones_bd) * (1.0 / HD_A)
    dlt = o - mu
    var = _headsum(dlt * dlt, ones_bd) * (1.0 / HD_A)
    on = dlt * lax.rsqrt(var + GN_EPS) * lnw_ref[...] + lnb_ref[...]
    bonus = _headsum(r_ref[...] * (kdf_ref[...] + kdb_ref[...]) * rk_ref[...], ones_bd) * v_ref[...]
    o_ref[...] = ((on + bonus) * g_ref[...]).astype(o_ref.dtype)


def _rwkv_post(of, ob, y, row0, m, kdf, kdb, g, prm, l, dst, tr=256):
    tr = min(tr, m)
    r0 = row0 // tr
    rs = pl.BlockSpec((tr, GW), lambda i: (i, 0))
    ps = pl.BlockSpec((None, 1, GW), lambda i: (l, 0, 0))
    return _into(
        dst, _rwkv_post_kernel,
        grid=(m // tr,),
        in_specs=[rs, rs, pl.BlockSpec((tr, GW), lambda i: (r0 + i, 0)),
                  pl.BlockSpec((tr, GW), lambda i: (r0 + i, 2)), rs, rs, rs, ps, ps, ps],
        out_specs=pl.BlockSpec((tr, GW), lambda i: (r0 + i, 0)),
        compiler_params=_cp("arbitrary"),
        name="rwkv_post",
    )(of, ob, y, y, kdf, kdb, g, prm["ln_w"], prm["ln_b"], prm["r_k"])


def _rwkv_mixer(y, row0, nb, t, prm, l, s0f, s0b, dst):
    m = nb * t
    kk, wf, wb, bf, bb, kdf, kdb, g = _rwkv_prep(y, row0, m, prm, l)
    of, ob, sf, sb = _rwkv_scan(y, row0, nb, t, (kk, wf, wb, bf, bb, kdf, kdb), s0f, s0b)
    ya = _rwkv_post(of, ob, y, row0, m, kdf, kdb, g, prm, l, dst)
    return ya, sf, sb


def _state_in(s):
    b, h, v, k = s.shape
    return s.reshape(b, h // 2, 2, v, k).transpose(0, 1, 3, 2, 4).reshape(b, h // 2 * v, 2 * k)


def _state_out(s):
    b = s.shape[0]
    hp = s.shape[1] // HD_A
    return s.reshape(b, hp, HD_A, 2, HD_A).transpose(0, 1, 3, 2, 4).reshape(b, 2 * hp, HD_A, HD_A)


def _rope_tables(t):
    pos = jnp.arange(t)
    row = (pos // GRID_W).astype(F32)
    col = (pos % GRID_W).astype(F32)
    lane = jnp.arange(LANES)
    dd = lane % DK_B
    half = DK_B // 2
    nf = half // 2
    part = dd // half
    ii = dd % half
    inv = ROPE_BASE ** (-jnp.arange(nf, dtype=F32) / nf)
    p = jnp.where(part[None, :] == 0, row[:, None], col[:, None])
    ang = p * inv[ii % nf][None, :]
    cos, sin = jnp.cos(ang), jnp.sin(ang)
    first = (ii < nf)[None, :]
    return cos, jnp.where(first, -sin, 0.0), jnp.where(first, 0.0, sin), nf


def _rope(x, cos, s1, s2, nf):
    return x * cos + pltpu.roll(x, LANES - nf, 1) * s1 + pltpu.roll(x, nf, 1) * s2


def _split_heads(q, width):
    lane = lax.broadcasted_iota(jnp.int32, q.shape, 1)
    return jnp.concatenate([jnp.where(lane < width, q, 0.0), jnp.where(lane >= width, q, 0.0)],
                           axis=0).astype(BF16)


def _lam(lam_ref, lam_init):
    lv = lam_ref[...]
    a = jnp.sum(lv[0:1] * lv[1:2], axis=-1, keepdims=True)
    b = jnp.sum(lv[2:3] * lv[3:4], axis=-1, keepdims=True)
    return jnp.exp(a) - jnp.exp(b) + lam_init


def _diff_ctx_kernel(q_ref, k_ref, v_ref, lam_ref, g_ref, o_ref, ko_ref, vo_ref, *, lam_init):
    t = q_ref.shape[0]
    lam = _lam(lam_ref, lam_init)
    for hh in range(q_ref.shape[1] // LANES):
        cols = slice(hh * LANES, (hh + 1) * LANES)
        k = k_ref[:, cols]
        v = v_ref[:, cols]
        ko_ref[hh] = k
        vo_ref[hh] = v
        s = _dot_nt(_split_heads(q_ref[:, cols] * (DK_B ** -0.5), DK_B), k.astype(BF16))
        e = jnp.exp(s - jnp.max(s, axis=-1, keepdims=True))
        p = e * (1.0 / jnp.sum(e, axis=-1, keepdims=True))
        o = _dot((p[:t] - lam * p[t:]).astype(BF16), v.astype(BF16))
        ms = jnp.mean(o * o, axis=-1, keepdims=True)
        o_ref[:, cols] = (o * lax.rsqrt(ms + EPS) * g_ref[...] * (1.0 - lam_init)).astype(o_ref.dtype)


def _diff_ctx(y, row0, nb, t, lam_all, g_all, l, lam_init, col0, dst, dst_k, dst_v, hps=4):
    nh = GW // LANES
    rb = row0 // t
    wb = hps * LANES
    cq, ck, cv = col0 // wb, (col0 + GW) // wb, (col0 + 2 * GW) // wb
    ysp = lambda cb: pl.BlockSpec((t, wb), lambda b, h: (rb + b, cb + h))
    cspec = pl.BlockSpec((None, None, hps, t, LANES), lambda b, h: (b, l, h, 0, 0))
    return _into(
        [dst, dst_k, dst_v], functools.partial(_diff_ctx_kernel, lam_init=lam_init),
        grid=(nb, nh // hps),
        in_specs=[ysp(cq), ysp(ck), ysp(cv),
                  pl.BlockSpec((None, 4, DK_B), lambda b, h: (l, 0, 0)),
                  pl.BlockSpec((None, 1, LANES), lambda b, h: (l, 0, 0))],
        out_specs=[pl.BlockSpec((t, wb), lambda b, h: (rb + b, h)), cspec, cspec],
        compiler_params=_cp("arbitrary", "arbitrary"),
        name="diff_ctx",
    )(y, y, y, lam_all, g_all)


def _diff_lat_kernel(q_ref, k_ref, v_ref, ck_ref, cv_ref, cos_ref, s1_ref, s2_ref, lam_ref, g_ref,
                     o_ref, kb_ref, vb_ref, *, lam_init, tq, nf):
    t = q_ref.shape[0]
    cos, s1, s2 = cos_ref[...], s1_ref[...], s2_ref[...]
    kb_ref[...] = _rope(k_ref[...], cos, s1, s2, nf).astype(BF16)
    vb_ref[...] = v_ref[...].astype(BF16)
    ckb = ck_ref[...].astype(BF16)
    cvb = cv_ref[...].astype(BF16)
    lam = _lam(lam_ref, lam_init)
    gsc = g_ref[...] * (1.0 - lam_init)

    def blk(i, carry):
        rows = pl.ds(pl.multiple_of(i * tq, tq), tq)
        q = _rope(q_ref[rows, :] * (DK_B ** -0.5), cos_ref[rows, :], s1_ref[rows, :], s2_ref[rows, :], nf)
        q2 = _split_heads(q, DK_B)
        sc = _dot_nt(q2, ckb)
        sl = _dot_nt(q2, kb_ref[...])
        mx = jnp.maximum(jnp.max(sc, axis=-1, keepdims=True), jnp.max(sl, axis=-1, keepdims=True))
        ec = jnp.exp(sc - mx)
        el = jnp.exp(sl - mx)
        rz = 1.0 / (jnp.sum(ec, axis=-1, keepdims=True) + jnp.sum(el, axis=-1, keepdims=True))
        pc = ec * rz
        plat = el * rz
        o = (_dot((pc[:tq] - lam * pc[tq:]).astype(BF16), cvb)
             + _dot((plat[:tq] - lam * plat[tq:]).astype(BF16), vb_ref[...]))
        ms = jnp.mean(o * o, axis=-1, keepdims=True)
        o_ref[rows, :] = (o * lax.rsqrt(ms + EPS) * gsc).astype(o_ref.dtype)
        return carry

    lax.fori_loop(0, t // tq, blk, 0)


def _diff_lat(y, row0, nb, t, cache_k, cache_v, lam_all, g_all, l, lam_init, col0, dst, tq=256):
    nh = GW // LANES
    rb = row0 // t
    tq = min(tq, t)
    cq, ck, cv = col0 // LANES, (col0 + GW) // LANES, (col0 + 2 * GW) // LANES
    ysp = lambda cb: pl.BlockSpec((t, LANES), lambda b, h: (rb + b, cb + h))
    past = cache_k.shape[3]
    csp = pl.BlockSpec((None, None, None, past, LANES), lambda b, h: (b, l, h, 0, 0))
    cos, s1, s2, nf = _rope_tables(t)
    tsp = pl.BlockSpec((t, LANES), lambda b, h: (0, 0))
    return _into(
        dst, functools.partial(_diff_lat_kernel, lam_init=lam_init, tq=tq, nf=nf),
        grid=(nb, nh),
        in_specs=[ysp(cq), ysp(ck), ysp(cv), csp, csp, tsp, tsp, tsp,
                  pl.BlockSpec((None, 4, DK_B), lambda b, h: (l, 0, 0)),
                  pl.BlockSpec((None, 1, LANES), lambda b, h: (l, 0, 0))],
        out_specs=pl.BlockSpec((t, LANES), lambda b, h: (rb + b, h)),
        scratch_shapes=[pltpu.VMEM((t, LANES), BF16), pltpu.VMEM((t, LANES), BF16)],
        compiler_params=_cp("arbitrary", "arbitrary"),
        name="diff_lat",
    )(y, y, y, cache_k, cache_v, cos, s1, s2, lam_all, g_all)


def _na_ctx_kernel(q_ref, k_ref, v_ref, o_ref, ko_ref, vo_ref):
    t = q_ref.shape[0]
    lane = lax.broadcasted_iota(jnp.int32, (t, LANES), 1)
    for pp in range(q_ref.shape[1] // LANES):
        cols = slice(pp * LANES, (pp + 1) * LANES)
        k = k_ref[:, cols]
        v = v_ref[:, cols]
        for h in range(2):
            ko_ref[2 * pp + h] = k[:, h * HD_C:(h + 1) * HD_C]
            vo_ref[2 * pp + h] = v[:, h * HD_C:(h + 1) * HD_C]
        s = _dot_nt(_split_heads(q_ref[:, cols] * (HD_C ** -0.5), HD_C), k.astype(BF16))
        e = jnp.exp(s - jnp.max(s, axis=-1, keepdims=True))
        p = e * (1.0 / jnp.sum(e, axis=-1, keepdims=True))
        o2 = _dot(p.astype(BF16), v.astype(BF16))
        o_ref[:, cols] = jnp.where(lane < HD_C, o2[:t], o2[t:]).astype(o_ref.dtype)


def _na_ctx(y, row0, nb, t, l, col0, dst, dst_k, dst_v, pps=4):
    npair = GW // LANES
    rb = row0 // t
    wb = pps * LANES
    cq, ck, cv = col0 // wb, (col0 + GW) // wb, (col0 + 2 * GW) // wb
    ysp = lambda cb: pl.BlockSpec((t, wb), lambda b, p: (rb + b, cb + p))
    cspec = pl.BlockSpec((None, None, 2 * pps, t, HD_C), lambda b, p: (b, l, p, 0, 0))
    return _into(
        [dst, dst_k, dst_v], _na_ctx_kernel,
        grid=(nb, npair // pps),
        in_specs=[ysp(cq), ysp(ck), ysp(cv)],
        out_specs=[pl.BlockSpec((t, wb), lambda b, p: (rb + b, p)), cspec, cspec],
        compiler_params=_cp("arbitrary", "arbitrary"),
        name="na_ctx",
    )(y, y, y)


def _na_bias_kernel(rpb_ref, o_ref, t_ref):
    ww = lax.broadcasted_iota(jnp.int32, (GRID_W, GRID_W), 0)
    jj = lax.broadcasted_iota(jnp.int32, (GRID_W, GRID_W), 1)
    c0 = jnp.clip(ww - WIN_C // 2, 0, GRID_W - WIN_C)
    allowed = (jj >= c0) & (jj < c0 + WIN_C)
    diff = jj - ww + (WIN_C - 1)
    for dr in range(2 * WIN_R - 1):
        acc = jnp.full((GRID_W, GRID_W), NEG, F32)
        for dc in range(2 * WIN_C - 1):
            acc = jnp.where(diff == dc, rpb_ref[dr, dc], acc)
        t_ref[dr] = jnp.where(allowed, acc, NEG)
    for pat in range(WIN_R):
        for i in range(WIN_R):
            o_ref[pat, :, i * GRID_W:(i + 1) * GRID_W] = t_ref[i - pat + WIN_R - 1]


def _na_bias_table(rpb):
    nl, nh, ndr, ndc = rpb.shape
    return pl.pallas_call(
        _na_bias_kernel,
        grid=(nl, nh),
        in_specs=[pl.BlockSpec((None, None, ndr, ndc), lambda l, h: (l, h, 0, 0), memory_space=pltpu.SMEM)],
        out_specs=pl.BlockSpec((None, None, WIN_R, GRID_W, WIN_R * GRID_W), lambda l, h: (l, h // 2, 0, h % 2, 0)),
        out_shape=jax.ShapeDtypeStruct((nl, nh // 2, WIN_R, 2 * GRID_W, WIN_R * GRID_W), F32),
        scratch_shapes=[pltpu.VMEM((ndr, GRID_W, GRID_W), F32)],
        compiler_params=_cp("arbitrary", "arbitrary"),
        name="na_bias",
    )(rpb)


def _na_lat_kernel(q_ref, k_ref, v_ref, ck_ref, cv_ref, bias_ref, o_ref, kb_ref, vb_ref):
    t = q_ref.shape[0]
    rows = t // GRID_W
    kr = min(WIN_R, rows)
    kb_ref[...] = k_ref[...].astype(BF16)
    vb_ref[...] = v_ref[...].astype(BF16)
    ckb = ck_ref[...].astype(BF16)
    cvb = cv_ref[...].astype(BF16)
    lane = lax.broadcasted_iota(jnp.int32, (GRID_W, LANES), 1)

    def row_step(r, carry):
        r0 = jnp.clip(r - kr // 2, 0, rows - kr)
        pat = r - r0
        qrows = pl.ds(pl.multiple_of(r * GRID_W, GRID_W), GRID_W)
        wrows = pl.ds(pl.multiple_of(r0 * GRID_W, GRID_W), kr * GRID_W)
        q2 = _split_heads(q_ref[qrows, :] * (HD_C ** -0.5), HD_C)
        sw = _dot_nt(q2, kb_ref[wrows, :]) + bias_ref[pat]
        sc = _dot_nt(q2, ckb)
        mx = jnp.maximum(jnp.max(sw, axis=-1, keepdims=True), jnp.max(sc, axis=-1, keepdims=True))
        ew = jnp.exp(sw - mx)
        ec = jnp.exp(sc - mx)
        rz = 1.0 / (jnp.sum(ew, axis=-1, keepdims=True) + jnp.sum(ec, axis=-1, keepdims=True))
        o2 = _dot((ew * rz).astype(BF16), vb_ref[wrows, :]) + _dot((ec * rz).astype(BF16), cvb)
        o_ref[qrows, :] = jnp.where(lane < HD_C, o2[:GRID_W], o2[GRID_W:]).astype(o_ref.dtype)
        return carry

    lax.fori_loop(0, rows, row_step, 0, unroll=2)


def _na_lat(y, row0, nb, t, ctx_k, ctx_v, bias, l, col0, dst):
    npair = GW // LANES
    rb = row0 // t
    cq, ck, cv = col0 // LANES, (col0 + GW) // LANES, (col0 + 2 * GW) // LANES
    ysp = lambda cb: pl.BlockSpec((t, LANES), lambda b, p: (rb + b, cb + p))
    past = ctx_k.shape[3]
    csp = pl.BlockSpec((None, None, None, past, LANES), lambda b, p: (b, l, p, 0, 0))
    bsp = pl.BlockSpec((None, None, WIN_R, 2 * GRID_W, WIN_R * GRID_W), lambda b, p: (l, p, 0, 0, 0))
    return _into(
        dst, _na_lat_kernel,
        grid=(nb, npair),
        in_specs=[ysp(cq), ysp(ck), ysp(cv), csp, csp, bsp],
        out_specs=pl.BlockSpec((t, LANES), lambda b, p: (rb + b, p)),
        scratch_shapes=[pltpu.VMEM((t, LANES), BF16), pltpu.VMEM((t, LANES), BF16)],
        compiler_params=_cp("arbitrary", "arbitrary"),
        name="na_lat",
    )(y, y, y, ctx_k, ctx_v, bias)


def _hy_positions(length):
    t = jnp.linspace(0.0, 1.0, length, dtype=F32)[:, None]
    ang = 2.0 * math.pi * jnp.arange(length, dtype=F32) / length
    bands = jnp.linspace(1e-4, HY_BANDS - 1, HY_BANDS, dtype=F32)
    z = jnp.concatenate([t, jnp.cos(ang[:, None] * bands[None, :]),
                         -jnp.sin(ang[:, None] * bands[None, :])], axis=-1)
    z = jnp.pad(z, ((0, 0), (0, LANES - z.shape[1])))
    mn = math.log(HY_TARGET) / HY_SLOW
    mx = math.log(HY_TARGET) / HY_FAST
    deltas = jnp.abs(jnp.linspace(mn, mx, GW, dtype=F32))
    decay = jnp.exp(-t * deltas[None, :])
    return z, decay


def _hy_filter_kernel(z_ref, dec_ref, w1_ref, b1_ref, f1_ref, w2_ref, b2_ref, f2_ref, w3_ref, o_ref):
    hid = jnp.sin(f1_ref[...] * (_dot3(z_ref[...], w1_ref[...]) + b1_ref[...]))
    hid = jnp.sin(f2_ref[...] * (_dot3(hid, w2_ref[...]) + b2_ref[...]))
    h = _dot3(hid, w3_ref[...])
    dec = dec_ref[...]
    first = (lax.broadcasted_iota(jnp.int32, dec.shape, 0) + pl.program_id(0) * dec.shape[0]) == 0
    for o in range(h.shape[1] // (2 * GW)):
        hf = h[:, (2 * o) * GW:(2 * o + 1) * GW] * dec
        hb = jnp.where(first, 0.0, h[:, (2 * o + 1) * GW:(2 * o + 2) * GW] * dec)
        o_ref[:, (2 * o) * GW:(2 * o + 1) * GW] = hf + hb
        o_ref[:, (2 * o + 1) * GW:(2 * o + 2) * GW] = hf - hb


def _hy_filters(length, prm, l):
    z, decay = _hy_positions(length)
    n3 = prm["w3"].shape[2]
    hid = prm["w2"].shape[1]
    tl = min(length, 256)
    ps = lambda shape: pl.BlockSpec((None,) + shape, lambda i: (l,) + (0,) * len(shape))
    return pl.pallas_call(
        _hy_filter_kernel,
        grid=(length // tl,),
        in_specs=[pl.BlockSpec((tl, LANES), lambda i: (i, 0)), pl.BlockSpec((tl, GW), lambda i: (i, 0)),
                  ps((LANES, hid)), ps((1, hid)), ps((1, hid)), ps((hid, hid)), ps((1, hid)), ps((1, hid)),
                  ps((hid, n3))],
        out_specs=pl.BlockSpec((tl, n3), lambda i: (i, 0)),
        out_shape=jax.ShapeDtypeStruct((length, n3), F32),
        compiler_params=_cp("arbitrary"),
        name="hy_filter",
    )(z, decay, prm["w1p"], prm["b1"], prm["f1"], prm["w2"], prm["b2"], prm["f2"], prm["w3"])


def _dft_mats(length):
    n2 = 2 * length
    k = jnp.arange(length, dtype=jnp.int32)
    kn = (k[:, None] * k[None, :]) % n2
    ang = kn.astype(F32) * (2.0 * math.pi / n2)
    c, s = jnp.cos(ang), jnp.sin(ang)
    alt = jnp.where(k % 2 == 0, 1.0, -1.0).astype(F32)
    f_im = jnp.where((k == 0)[:, None], alt[None, :], -s)
    fwd = jnp.concatenate([c, f_im], axis=0).astype(BF16)
    ck = jnp.where(k == 0, 1.0, 2.0)[None, :]
    g_re = c * ck / n2
    g_im = jnp.where((k == 0)[None, :], alt[:, None] / n2, -2.0 * s / n2)
    inv = jnp.concatenate([g_re, g_im], axis=1).astype(BF16)
    return fwd, inv


def _mm_plain_kernel(a_ref, w_ref, o_ref):
    o_ref[...] = _dot(a_ref[...], w_ref[...].astype(BF16))


def _hy_spectrum(fwd, hk, tm=512, tn=512):
    m, k = fwd.shape
    n = hk.shape[1]
    tm, tn = min(tm, m), min(tn, n)
    return pl.pallas_call(
        _mm_plain_kernel,
        grid=(m // tm, n // tn),
        in_specs=[pl.BlockSpec((tm, k), lambda i, j: (i, 0)), pl.BlockSpec((k, tn), lambda i, j: (0, j))],
        out_specs=pl.BlockSpec((tm, tn), lambda i, j: (i, j)),
        out_shape=jax.ShapeDtypeStruct((m, n), F32),
        compiler_params=_cp("arbitrary", "arbitrary"),
        name="hy_spectrum",
    )(fwd, hk)


def _short_conv_kernel(u_ref, w_ref, b_ref, o_ref):
    u = u_ref[...]
    t = u.shape[0]
    rowi = lax.broadcasted_iota(jnp.int32, u.shape, 0)
    prev = jnp.where(rowi == 0, 0.0, pltpu.roll(u, 1, 0))
    nxt = jnp.where(rowi == t - 1, 0.0, pltpu.roll(u, t - 1, 0))
    o_ref[...] = prev * w_ref[0:1, :] + u * w_ref[1:2, :] + nxt * w_ref[2:3, :] + b_ref[...]


def _short_conv(y, row0, nb, t, w_all, b_all, l, col0, width, tc=512):
    rb = row0 // t
    cb = col0 // tc
    return pl.pallas_call(
        _short_conv_kernel,
        grid=(nb, width // tc),
        in_specs=[pl.BlockSpec((t, tc), lambda b, j: (rb + b, cb + j)),
                  pl.BlockSpec((None, 3, tc), lambda b, j: (l, 0, j)),
                  pl.BlockSpec((None, 1, tc), lambda b, j: (l, 0, j))],
        out_specs=pl.BlockSpec((t, tc), lambda b, j: (b, j)),
        out_shape=jax.ShapeDtypeStruct((nb * t, width), F32),
        compiler_params=_cp("arbitrary", "arbitrary"),
        name="short_conv",
    )(y, w_all, b_all)


def _hy_fwd_kernel(fre_ref, fim_ref, z_ref, kre_ref, kny_ref, kim_ref, o_ref):
    zb = z_ref[...].astype(BF16)
    re = _dot(fre_ref[...], zb)
    im = _dot(fim_ref[...], zb)
    tm = re.shape[0]
    dc = (lax.broadcasted_iota(jnp.int32, re.shape, 0) + pl.program_id(1) * tm) == 0
    kre = kre_ref[...]
    kim = jnp.where(dc, kny_ref[...], kim_ref[...])
    o_ref[0] = jnp.where(dc, re * kre, re * kre - im * kim).astype(o_ref.dtype)
    o_ref[1] = jnp.where(dc, im * kim, re * kim + im * kre).astype(o_ref.dtype)


def _hy_fwd(fwd, z, zcol, nb, t, kspec, o, tm=512, tn=512):
    tm, tn = min(tm, t), min(tn, GW)
    ni = t // tm
    zc = zcol // tn
    cs, cd = (2 * o) * GW // tn, (2 * o + 1) * GW // tn
    return pl.pallas_call(
        _hy_fwd_kernel,
        grid=(nb, ni, GW // tn),
        in_specs=[pl.BlockSpec((tm, t), lambda b, i, j: (i, 0)),
                  pl.BlockSpec((tm, t), lambda b, i, j: (ni + i, 0)),
                  pl.BlockSpec((t, tn), lambda b, i, j: (b, zc + j)),
                  pl.BlockSpec((tm, tn), lambda b, i, j: (i, cs + j)),
                  pl.BlockSpec((tm, tn), lambda b, i, j: (ni + i, cs + j)),
                  pl.BlockSpec((tm, tn), lambda b, i, j: (ni + i, cd + j))],
        out_specs=pl.BlockSpec((None, 2, tm, tn), lambda b, i, j: (b, 0, i, j)),
        out_shape=jax.ShapeDtypeStruct((nb, 2, t, GW), BF16),
        compiler_params=_cp("arbitrary", "arbitrary", "arbitrary"),
        name="hy_fwd",
    )(fwd, fwd, z, kspec, kspec, kspec)


def _hy_inv_kernel(g_ref, s_ref, zin_ref, gate_ref, skip_ref, o_ref):
    yv = _dot(g_ref[...], s_ref[...])
    o_ref[...] = (gate_ref[...] * (yv + skip_ref[...] * zin_ref[...])).astype(o_ref.dtype)


def _hy_inv(inv, spec, z, zcol, gate, gcol, skip_all, l, o, nb, t, dst, row0=0, tm=512, tn=512):
    tm, tn = min(tm, t), min(tn, GW)
    ni = t // tm
    zc, gc = zcol // tn, gcol // tn
    ro = row0 // tm
    return _into(
        dst, _hy_inv_kernel,
        grid=(nb, ni, GW // tn),
        in_specs=[pl.BlockSpec((tm, 2 * t), lambda b, i, j: (i, 0)),
                  pl.BlockSpec((None, 2 * t, tn), lambda b, i, j: (b, 0, j)),
                  pl.BlockSpec((tm, tn), lambda b, i, j: (b * ni + i, zc + j)),
                  pl.BlockSpec((tm, tn), lambda b, i, j: (b * ni + i, gc + j)),
                  pl.BlockSpec((None, None, 1, tn), lambda b, i, j: (l, o, 0, j))],
        out_specs=pl.BlockSpec((tm, tn), lambda b, i, j: (ro + b * ni + i, j)),
        compiler_params=_cp("arbitrary", "arbitrary", "arbitrary"),
        name="hy_inv",
    )(inv, spec.reshape(nb, 2 * t, GW), z, gate, skip_all)


def _hyena_mixer(y, row0, nb, t, prm, l, col0, consts, dst):
    fwd, inv, kspec = consts
    u = _short_conv(y, row0, nb, t, prm["sconv_w"], prm["sconv_b"], l, col0, 3 * GW)
    spec = _hy_fwd(fwd, u, 0, nb, t, kspec, 0)
    z = _hy_inv(inv, spec, u, 0, u, GW, prm["skip"], l, 0, nb, t, jax.ShapeDtypeStruct((nb * t, GW), F32))
    spec = _hy_fwd(fwd, z, 0, nb, t, kspec, 1)
    return _hy_inv(inv, spec, z, 0, u, 2 * GW, prm["skip"], l, 1, nb, t, dst, row0)


def _hyena_consts(t, prm, l):
    fwd, inv = _dft_mats(t)
    return fwd, inv, _hy_spectrum(fwd, _hy_filters(t, prm, l))


def _permute_w_in(w_in):
    gw = GW
    big = w_in[..., :3 * gw]
    o = 3 * gw
    small = w_in[..., o:o + 416]
    rest = w_in[..., o + 416:]
    pad = jnp.zeros(w_in.shape[:-1] + (R_SMALL - 416,), w_in.dtype)
    return jnp.concatenate([big, rest, small, pad], axis=-1).astype(BF16)


def _pad_rows(w, rows, at=0):
    pad = [(0, 0)] * (w.ndim - 2) + [(at, rows - at - w.shape[-2]), (0, 0)]
    return jnp.pad(w, pad)


def kernel(x_prompt, x_sample, cache_diff_k, cache_diff_v, cache_na_k, cache_na_v, state_rwkv_fwd,
           state_rwkv_bwd, c, c_ctx, norm1_g, norm2_g, w_mod, b_mod, w_in, rwkv_w0, rwkv_w2, rwkv_a0,
           rwkv_a2, rwkv_g2, rwkv_k_k, rwkv_k_a, rwkv_r_k, rwkv_ln_w, rwkv_ln_b, diff_lam,
           diff_subln_g, na_rpb, hy_sconv_w, hy_sconv_b, hy_f_w1, hy_f_b1, hy_f_freq1, hy_f_w2,
           hy_f_b2, hy_f_freq2, hy_f_w3, hy_skip, w_out, w_up, w_down, final_g):
    bp, tp, d = x_prompt.shape
    bs, ts, _ = x_sample.shape
    nl = w_in.shape[0]
    mp, ms = bp * tp, bs * ts
    assert mp % 1024 == 0 and ts % 1024 == 0 and bs + 1 <= 8
    grp = _group_fn(mp, ts)

    mt = mp + ms
    xp, xs = x_prompt.reshape(mp, d), x_sample.reshape(ms, d)
    cv = jnp.concatenate([c_ctx[None], c, jnp.zeros((8 - 1 - bs, d), F32)], axis=0)
    mod = _modulation(cv, w_mod, b_mod).reshape(nl, 8, N_MOD, 1, d)

    w_in_b = _permute_w_in(w_in)
    w_out_b = w_out.astype(BF16)
    w_up_b = w_up.astype(BF16)
    w_down_b = w_down.astype(BF16)
    row = lambda p: p.reshape(nl, 1, -1)
    rw = {
        "w0": rwkv_w0, "a0": rwkv_a0,
        "w2p": jnp.stack([_pad_rows(rwkv_w2[:, 0], LANES, 0), _pad_rows(rwkv_w2[:, 1], LANES, 64)], axis=1),
        "a2p": jnp.stack([_pad_rows(rwkv_a2[:, 0], LANES, 0), _pad_rows(rwkv_a2[:, 1], LANES, 64)], axis=1),
        "g2p": _pad_rows(rwkv_g2, 2 * LANES, 0),
        "k_k": row(rwkv_k_k), "k_a": row(rwkv_k_a), "r_k": row(rwkv_r_k),
        "ln_w": row(rwkv_ln_w), "ln_b": row(rwkv_ln_b),
    }
    hy = {
        "sconv_w": hy_sconv_w, "sconv_b": hy_sconv_b.reshape(nl, 1, -1),
        "w1p": _pad_rows(hy_f_w1, LANES, 0), "b1": row(hy_f_b1), "f1": row(hy_f_freq1),
        "w2": hy_f_w2, "b2": row(hy_f_b2), "f2": row(hy_f_freq2), "w3": hy_f_w3,
        "skip": hy_skip.reshape(nl, 2, 1, GW),
    }
    subln = diff_subln_g.reshape(nl, 1, -1)
    norm1 = norm1_g.reshape(nl, 1, d)
    norm2 = norm2_g.reshape(nl, 1, d)
    na_bias = _na_bias_table(na_rpb)
    pair = lambda a: a.reshape(a.shape[0], a.shape[1], a.shape[2] // 2, 2, a.shape[3], a.shape[4]) \
        .transpose(0, 1, 2, 4, 3, 5).reshape(a.shape[0], a.shape[1], a.shape[2] // 2, a.shape[3], 2 * a.shape[4])
    na_ck, na_cv = pair(cache_na_k), pair(cache_na_v)
    zero_state = jnp.zeros((bp, GW // LANES * HD_A, LANES), F32)

    c_b, c_c, c_d = 3 * GW, 6 * GW, 9 * GW
    act = lambda dtype, width=d: jax.ShapeDtypeStruct((mt, width), dtype)
    xsrc = ((xp, 0, 0, mp), (xs, 0, mp, ms))
    dk = jax.ShapeDtypeStruct((bp, nl, GW // LANES, tp, LANES), F32)
    nk = jax.ShapeDtypeStruct((bp, nl, GW // HD_C, tp, HD_C), F32)
    dv, nv = dk, nk
    states = [[], []]
    for l in range(nl):
        lam_init = 0.8 - 0.6 * math.exp(-0.3 * l)
        h = act(BF16)
        for xa, xrow0, orow0, m in xsrc:
            h = _norm_mod(xa, xrow0, m, norm1, mod, l, 1, 0, grp, h, orow0)
        y = _mm_fullk([h], w_in_b, l, act(F32, w_in_b.shape[2]), 0, mt, name="proj_in")

        ya, sf, sb = _rwkv_mixer(y, 0, bp, tp, rw, l, zero_state, zero_state, act(BF16, GW))
        yb, dk, dv = _diff_ctx(y, 0, bp, tp, diff_lam, subln, l, lam_init, c_b, act(BF16, GW), dk, dv)
        yc, nk, nv = _na_ctx(y, 0, bp, tp, l, c_c, act(BF16, GW), nk, nv)
        yd = _hyena_mixer(y, 0, bp, tp, hy, l, c_d, _hyena_consts(tp, hy, l), act(BF16, GW))
        ya, _, _ = _rwkv_mixer(y, mp, bs, ts, rw, l, _state_in(state_rwkv_fwd[:, l]),
                               _state_in(state_rwkv_bwd[:, l]), ya)
        yb = _diff_lat(y, mp, bs, ts, cache_diff_k, cache_diff_v, diff_lam, subln, l, lam_init, c_b, yb)
        yc = _na_lat(y, mp, bs, ts, na_ck, na_cv, na_bias, l, c_c, yc)
        yd = _hyena_mixer(y, mp, bs, ts, hy, l, c_d, _hyena_consts(ts, hy, l), yd)

        x = act(F32)
        for xa, xrow0, orow0, m in xsrc:
            x = _mm_fullk([ya, yb, yc, yd], w_out_b, l, x, orow0, m, epilogue="resid",
                          resid=(xa, xrow0, mod, 2, grp), name="proj_out")
        h = _norm_mod(x, 0, mt, norm2, mod, l, 4, 3, grp, act(BF16), 0)
        u = _mm_fullk([h], w_up_b, l, act(BF16, w_up_b.shape[2]), 0, mt, epilogue="relu2", name="ffn_up")
        x = _mm_kloop_resid(u, w_down_b, l, x, mod, 5, grp)
        xsrc = ((x, 0, 0, mt),)
        states[0].append(_state_out(sf))
        states[1].append(_state_out(sb))

    return (_rmsnorm(x, 0, mp, final_g).reshape(bp, tp, d), _rmsnorm(x, mp, ms, final_g).reshape(bs, ts, d),
            dk, dv, nk, nv, jnp.stack(states[0], axis=1), jnp.stack(states[1], axis=1))
```

```python
import functools
import math

import jax
import jax.numpy as jnp
import numpy as np
from jax import lax
from jax.experimental import pallas as pl
from jax.experimental.pallas import tpu as pltpu

F32 = jnp.float32
BF16 = jnp.bfloat16

GRID_W = 64
HD_A = 64
DK_B = 64
HD_C = 64
WIN_R = 8
WIN_C = 16
GW = 1024
R_SMALL = 512
N_MOD = 6
GN_EPS = 64e-5
EPS = 1e-6
ROPE_BASE = 10000.0
HY_BANDS = 16
HY_TARGET = 1e-2
HY_FAST = 0.3
HY_SLOW = 1.5
NEG = -1e30
LANES = 128
VMEM_LIMIT = 56 * 1024 * 1024


def _cp(*sem):
    return pltpu.CompilerParams(dimension_semantics=sem, vmem_limit_bytes=VMEM_LIMIT)


def _into(dst, kernel, *, in_specs, out_specs, **kw):
    multi = isinstance(out_specs, (list, tuple))
    dsts = list(dst) if multi else [dst]
    shapes = [d if isinstance(d, jax.ShapeDtypeStruct) else jax.ShapeDtypeStruct(d.shape, d.dtype) for d in dsts]
    held = [(o, d) for o, d in enumerate(dsts) if not isinstance(d, jax.ShapeDtypeStruct)]
    n_in = len(in_specs)

    def body(*refs):
        kernel(*refs[:n_in], *refs[n_in + len(held):])

    call = pl.pallas_call(body, in_specs=[*in_specs] + [pl.BlockSpec(memory_space=pl.ANY)] * len(held),
                          out_specs=out_specs, out_shape=shapes if multi else shapes[0],
                          input_output_aliases={n_in + i: o for i, (o, _) in enumerate(held)}, **kw)
    return lambda *args: call(*args, *[d for _, d in held])


def _split_bf16(x):
    hi = x.astype(BF16)
    lo = (x - hi.astype(F32)).astype(BF16)
    return hi, lo


def _dot(a, b):
    return jnp.dot(a, b, preferred_element_type=F32)


def _dot_nt(a, b):
    return lax.dot_general(a, b, (((1,), (1,)), ((), ())), preferred_element_type=F32)


def _dot3(a, b):
    ah, al = _split_bf16(a)
    bh, bl = _split_bf16(b)
    return _dot(ah, bh) + _dot(al, bh) + _dot(ah, bl)


def _ones_bd(n=LANES, blk=HD_A):
    r = lax.broadcasted_iota(jnp.int32, (n, n), 0) // blk
    c = lax.broadcasted_iota(jnp.int32, (n, n), 1) // blk
    return jnp.where(r == c, 1.0, 0.0).astype(BF16)


def _headsum(x, ones_bd):
    hi, lo = _split_bf16(x)
    outs = []
    for g in range(x.shape[1] // LANES):
        sl = slice(g * LANES, (g + 1) * LANES)
        outs.append(_dot(hi[:, sl], ones_bd) + _dot(lo[:, sl], ones_bd))
    return jnp.concatenate(outs, axis=1) if len(outs) > 1 else outs[0]


def _group_fn(mp, ts):
    def g(row0):
        return jnp.where(row0 < mp, 0, 1 + (row0 - mp) // ts)
    return g


def _mod_kernel(c_ref, w_ref, b_ref, o_ref):
    c = c_ref[...]
    s = c * jax.nn.sigmoid(c)
    o_ref[...] = _dot3(s, w_ref[...]) + b_ref[...]


def _modulation(cv, w_mod, b_mod):
    nl, d, n = w_mod.shape
    tn = 512
    return pl.pallas_call(
        _mod_kernel,
        grid=(nl, n // tn),
        in_specs=[pl.BlockSpec((8, d), lambda l, j: (0, 0)),
                  pl.BlockSpec((None, d, tn), lambda l, j: (l, 0, j)),
                  pl.BlockSpec((None, 1, tn), lambda l, j: (l, 0, j))],
        out_specs=pl.BlockSpec((None, 8, tn), lambda l, j: (l, 0, j)),
        out_shape=jax.ShapeDtypeStruct((nl, 8, n), F32),
        compiler_params=_cp("arbitrary", "arbitrary"),
        name="modulation",
    )(cv, w_mod, b_mod.reshape(nl, 1, n))


def _norm_mod_kernel(x_ref, g_ref, sc_ref, sh_ref, o_ref):
    x = x_ref[...]
    ms = jnp.mean(x * x, axis=-1, keepdims=True)
    y = x * lax.rsqrt(ms + EPS) * g_ref[...]
    o_ref[...] = (y * (1.0 + sc_ref[...]) + sh_ref[...]).astype(o_ref.dtype)


def _norm_mod(x, xrow0, m, g_all, mod, l, which_sc, which_sh, grp, dst, orow0, tr=256):
    d = x.shape[1]
    tr = min(tr, m)
    xr, orr = xrow0 // tr, orow0 // tr
    mspec = lambda w: pl.BlockSpec((None, None, None, 1, d), lambda i: (l, grp(orow0 + i * tr), w, 0, 0))
    return _into(
        dst, _norm_mod_kernel,
        grid=(m // tr,),
        in_specs=[pl.BlockSpec((tr, d), lambda i: (xr + i, 0)),
                  pl.BlockSpec((None, 1, d), lambda i: (l, 0, 0)),
                  mspec(which_sc), mspec(which_sh)],
        out_specs=pl.BlockSpec((tr, d), lambda i: (orr + i, 0)),
        compiler_params=_cp("arbitrary"),
        name="norm_mod",
    )(x, g_all, mod, mod)


def _rmsnorm_kernel(x_ref, g_ref, o_ref):
    x = x_ref[...]
    ms = jnp.mean(x * x, axis=-1, keepdims=True)
    o_ref[...] = x * lax.rsqrt(ms + EPS) * g_ref[...]


def _rmsnorm(x, xrow0, m, g, tr=256):
    d = x.shape[1]
    tr = min(tr, m)
    xr = xrow0 // tr
    return pl.pallas_call(
        _rmsnorm_kernel,
        grid=(m // tr,),
        in_specs=[pl.BlockSpec((tr, d), lambda i: (xr + i, 0)),
                  pl.BlockSpec((1, d), lambda i: (0, 0))],
        out_specs=pl.BlockSpec((tr, d), lambda i: (i, 0)),
        out_shape=jax.ShapeDtypeStruct((m, d), F32),
        compiler_params=_cp("arbitrary"),
        name="final_norm",
    )(x, g.reshape(1, d))


def _mm_fullk_kernel(*refs, n_a, epilogue):
    a_refs = refs[:n_a]
    w_ref = refs[n_a]
    o_ref = refs[-1]
    acc = None
    k0 = 0
    for a_ref in a_refs:
        kw = a_ref.shape[1]
        part = _dot(a_ref[...], w_ref[k0:k0 + kw, :])
        acc = part if acc is None else acc + part
        k0 += kw
    if epilogue == "relu2":
        acc = jnp.square(jnp.maximum(acc, 0.0))
    elif epilogue == "resid":
        x_ref, gate_ref = refs[n_a + 1], refs[n_a + 2]
        acc = x_ref[...] + gate_ref[...] * acc
    o_ref[...] = acc.astype(o_ref.dtype)


def _mm_fullk(a_list, w_all, l, dst, row0, m, epilogue="none", resid=None, tm=1024, tn=512, name="mm"):
    k, n = w_all.shape[1], w_all.shape[2]
    tm, tn = min(tm, m), min(tn, n)
    rb = row0 // tm
    in_specs = [pl.BlockSpec((tm, a.shape[1]), lambda i, j: (rb + i, 0)) for a in a_list]
    in_specs.append(pl.BlockSpec((None, k, tn), lambda i, j: (l, 0, j)))
    args = list(a_list) + [w_all]
    if epilogue == "resid":
        x, xrow0, mod, which, grp = resid
        xb = xrow0 // tm
        in_specs.append(pl.BlockSpec((tm, tn), lambda i, j: (xb + i, j)))
        in_specs.append(pl.BlockSpec((None, None, None, 1, tn),
                                     lambda i, j: (l, grp(row0 + i * tm), which, 0, j)))
        args += [x, mod]
    return _into(
        dst, functools.partial(_mm_fullk_kernel, n_a=len(a_list), epilogue=epilogue),
        grid=(m // tm, n // tn),
        in_specs=in_specs,
        out_specs=pl.BlockSpec((tm, tn), lambda i, j: (rb + i, j)),
        compiler_params=_cp("arbitrary", "arbitrary"),
        name=name,
    )(*args)


def _mm_kloop_kernel(a_ref, w_ref, x_ref, gate_ref, o_ref, acc_ref):
    kk = pl.program_id(2)

    @pl.when(kk == 0)
    def _():
        acc_ref[...] = jnp.zeros_like(acc_ref)

    acc_ref[...] += _dot(a_ref[...], w_ref[...])

    @pl.when(kk == pl.num_programs(2) - 1)
    def _():
        o_ref[...] = x_ref[...] + gate_ref[...] * acc_ref[...]


def _mm_kloop_resid(a, w_all, l, x, mod, which, grp, tm=1024, tn=1024, tk=2048):
    m, k = a.shape
    n = w_all.shape[2]
    tm, tn, tk = min(tm, m), min(tn, n), min(tk, k)
    return pl.pallas_call(
        _mm_kloop_kernel,
        grid=(m // tm, n // tn, k // tk),
        in_specs=[pl.BlockSpec((tm, tk), lambda i, j, q: (i, q)),
                  pl.BlockSpec((None, tk, tn), lambda i, j, q: (l, q, j)),
                  pl.BlockSpec((tm, tn), lambda i, j, q: (i, j)),
                  pl.BlockSpec((None, None, None, 1, tn),
                               lambda i, j, q: (l, grp(i * tm), which, 0, j))],
        out_specs=pl.BlockSpec((tm, tn), lambda i, j, q: (i, j)),
        out_shape=jax.ShapeDtypeStruct((m, n), F32),
        scratch_shapes=[pltpu.VMEM((tm, tn), F32)],
        compiler_params=_cp("arbitrary", "arbitrary", "arbitrary"),
        name="ffn_down",
    )(a, w_all, x, mod)


def _softplus(z):
    return jnp.maximum(z, 0.0) + jnp.log(1.0 + jnp.exp(-jnp.abs(z)))


def _rwkv_prep_kernel(k_ref, sm_ref, w0_ref, w2_ref, a0_ref, a2_ref, g2_ref, kk_ref_p, ka_ref_p,
                      kk_o, wf_o, wb_o, bf_o, bb_o, kdf_o, kdb_o, g_o):
    ones_bd = _ones_bd()
    k = k_ref[...]
    kraw = k * kk_ref_p[...]
    ss = _headsum(kraw * kraw, ones_bd)
    kk = kraw / jnp.maximum(jnp.sqrt(ss), 1e-12)
    kk_o[...] = kk
    sm = sm_ref[...]
    wd = jnp.tanh(sm[:, 0:LANES])
    ad = sm[:, LANES:2 * LANES]
    gd = jax.nn.sigmoid(sm[:, 2 * LANES:4 * LANES])
    ka = ka_ref_p[...]
    for d, (w_o, b_o, kd_o) in enumerate(((wf_o, bf_o, kdf_o), (wb_o, bb_o, kdb_o))):
        wl = w0_ref[d:d + 1, :] + _dot3(wd, w2_ref[d])
        wlog = -_softplus(-wl) - 0.5
        w_o[...] = jnp.exp(-jnp.exp(wlog))
        a = jax.nn.sigmoid(a0_ref[d:d + 1, :] + _dot3(ad, a2_ref[d]))
        b_o[...] = kk * a
        kd_o[...] = k * (1.0 + (a - 1.0) * ka)
    g_o[...] = _dot3(gd, g2_ref[...])


def _rwkv_prep(y, row0, m, prm, l, tr=256):
    tr = min(tr, m)
    r0 = row0 // tr
    nsm = y.shape[1] // R_SMALL - 1
    out = jax.ShapeDtypeStruct((m, GW), F32)
    ospec = pl.BlockSpec((tr, GW), lambda i: (i, 0))
    pspec = lambda shape: pl.BlockSpec((None,) + shape, lambda i: (l,) + (0,) * len(shape))
    return pl.pallas_call(
        _rwkv_prep_kernel,
        grid=(m // tr,),
        in_specs=[pl.BlockSpec((tr, GW), lambda i: (r0 + i, 1)),
                  pl.BlockSpec((tr, R_SMALL), lambda i: (r0 + i, nsm)),
                  pspec((2, GW)), pspec((2, LANES, GW)), pspec((2, GW)), pspec((2, LANES, GW)),
                  pspec((2 * LANES, GW)), pspec((1, GW)), pspec((1, GW))],
        out_specs=[ospec] * 8,
        out_shape=[out] * 8,
        compiler_params=_cp("arbitrary"),
        name="rwkv_prep",
    )(y, y, prm["w0"], prm["w2p"], prm["a0"], prm["a2p"], prm["g2p"], prm["k_k"], prm["k_a"])


def _rwkv_scan_kernel(kk_f, r_f, v_f, w_f, b_f, kd_f, kk_b, r_b, v_b, w_b, b_b, kd_b,
                      s0f_ref, s0b_ref, of_ref, ob_ref, sf_ref, sb_ref, st_ref, red_ref, oacc_ref,
                      *, tc, gpc):
    c = pl.program_id(1)

    @pl.when(c == 0)
    def _():
        st_ref[0] = s0f_ref[...]
        st_ref[1] = s0b_ref[...]

    sub = 8
    ngrp = GW // LANES
    cr = gpc * HD_A
    nch = ngrp // gpc
    ones_bd = _ones_bd()
    rowi = lax.broadcasted_iota(jnp.int32, (cr, LANES), 0) % HD_A
    lane = lax.broadcasted_iota(jnp.int32, (cr, LANES), 1) % HD_A
    eye = jnp.where(rowi == lane, 1.0, 0.0).astype(BF16)
    dirs = ((kk_f, r_f, v_f, w_f, b_f, kd_f), (kk_b, r_b, v_b, w_b, b_b, kd_b))

    def bc(ref, base, rj, ch):
        r = ref[pl.ds(base, sub), pl.ds(ch * gpc * LANES, gpc * LANES)][rj:rj + 1, :]
        parts = [jnp.broadcast_to(r[:, q * LANES:(q + 1) * LANES], (HD_A, LANES)) for q in range(gpc)]
        return parts[0] if gpc == 1 else jnp.concatenate(parts, axis=0)

    def here(d, i, j):
        if d == 0:
            return pl.multiple_of(i * sub, sub), j
        return pl.multiple_of(tc - sub - i * sub, sub), sub - 1 - j

    def ahead(d, i, j):
        if j + 1 < sub:
            return here(d, i, j + 1)
        if d == 0:
            return pl.multiple_of(jnp.minimum(i * sub + sub, tc - sub), sub), 0
        return pl.multiple_of(jnp.maximum(tc - 2 * sub - i * sub, 0), sub), sub - 1

    for d, (kk_r, r_r, v_r, w_r, b_r, kd_r) in enumerate(dirs):
        base, rj = here(d, 0, 0)
        for ch in range(nch):
            rws = pl.ds(ch * cr, cr)
            sh = st_ref[d, rws, :].astype(BF16)
            lhs = jnp.concatenate([sh * bc(kk_r, base, rj, ch).astype(BF16),
                                   eye * bc(v_r, base, rj, ch).astype(BF16)], axis=0)
            red_ref[d, ch] = _dot(lhs, ones_bd)

    def step(i, carry):
        red = {(d, ch): red_ref[d, ch] for d in range(2) for ch in range(nch)}
        for j in range(sub):
            for d, (kk_r, r_r, v_r, w_r, b_r, kd_r) in enumerate(dirs):
                base, rj = here(d, i, j)
                nbase, nrj = ahead(d, i, j)
                for ch in range(nch):
                    rws = pl.ds(ch * cr, cr)
                    rd = red[(d, ch)]
                    s = (st_ref[d, rws, :] * bc(w_r, base, rj, ch) - rd[:cr] * bc(b_r, base, rj, ch)
                         + rd[cr:] * bc(kd_r, base, rj, ch))
                    st_ref[d, rws, :] = s
                    sh = s.astype(BF16)
                    lhs = jnp.concatenate([sh * bc(kk_r, nbase, nrj, ch).astype(BF16),
                                           eye * bc(v_r, nbase, nrj, ch).astype(BF16),
                                           sh * bc(r_r, base, rj, ch).astype(BF16)], axis=0)
                    out = _dot(lhs, ones_bd)
                    red[(d, ch)] = out[:2 * cr]
                    pltpu.store(oacc_ref.at[d, rws, :], out[2 * cr:], mask=lane == base + rj)
        for key, val in red.items():
            red_ref[key[0], key[1]] = val
        return carry

    lax.fori_loop(0, tc // sub, step, 0)

    for d, o_ref in enumerate((of_ref, ob_ref)):
        ot = oacc_ref[d].T
        top, bot = ot[:HD_A], ot[HD_A:]
        pieces = []
        for g in range(ngrp):
            pieces += [top[:, g * HD_A:(g + 1) * HD_A], bot[:, g * HD_A:(g + 1) * HD_A]]
        o_ref[...] = jnp.concatenate(pieces, axis=1)

    @pl.when(c == pl.num_programs(1) - 1)
    def _():
        sf_ref[...] = st_ref[0]
        sb_ref[...] = st_ref[1]


def _rwkv_scan(y, row0, nb, t, pre, s0f, s0b, tc=64, gpc=1):
    kk, wf, wb, bf, bb, kdf, kdb = pre
    assert tc == HD_A
    nc = t // tc
    yb = row0 // tc
    rows = GW // LANES * HD_A
    fw = lambda col: pl.BlockSpec((tc, GW), lambda b, c: (yb + b * nc + c, col))
    bw = lambda col: pl.BlockSpec((tc, GW), lambda b, c: (yb + b * nc + nc - 1 - c, col))
    fwp = pl.BlockSpec((tc, GW), lambda b, c: (b * nc + c, 0))
    bwp = pl.BlockSpec((tc, GW), lambda b, c: (b * nc + nc - 1 - c, 0))
    sspec = pl.BlockSpec((None, rows, LANES), lambda b, c: (b, 0, 0))
    oshape = jax.ShapeDtypeStruct((nb * t, GW), F32)
    sshape = jax.ShapeDtypeStruct((nb, rows, LANES), F32)
    return pl.pallas_call(
        functools.partial(_rwkv_scan_kernel, tc=tc, gpc=gpc),
        grid=(nb, nc),
        in_specs=[fwp, fw(0), fw(2), fwp, fwp, fwp, bwp, bw(0), bw(2), bwp, bwp, bwp, sspec, sspec],
        out_specs=[fwp, bwp, sspec, sspec],
        out_shape=[oshape, oshape, sshape, sshape],
        scratch_shapes=[pltpu.VMEM((2, rows, LANES), F32),
                        pltpu.VMEM((2, rows // (gpc * HD_A), 2 * gpc * HD_A, LANES), F32),
                        pltpu.VMEM((2, rows, LANES), F32)],
        compiler_params=_cp("arbitrary", "arbitrary"),
        name="rwkv_scan",
    )(kk, y, y, wf, bf, kdf, kk, y, y, wb, bb, kdb, s0f, s0b)


def _rwkv_post_kernel(of_ref, ob_ref, r_ref, v_ref, kdf_ref, kdb_ref, g_ref, lnw_ref, lnb_ref, rk_ref, o_ref):
    ones_bd = _ones_bd()
    o = of_ref[...] + ob_ref[...]
    mu = _headsum(o, ones_bd) * (1.0 / HD_A)
    dlt = o - mu
    var = _headsum(dlt * dlt, ones_bd) * (1.0 / HD_A)
    on = dlt * lax.rsqrt(var + GN_EPS) * lnw_ref[...] + lnb_ref[...]
    bonus = _headsum(r_ref[...] * (kdf_ref[...] + kdb_ref[...]) * rk_ref[...], ones_bd) * v_ref[...]
    o_ref[...] = ((on + bonus) * g_ref[...]).astype(o_ref.dtype)


def _rwkv_post(of, ob, y, row0, m, kdf, kdb, g, prm, l, dst, tr=256):
    tr = min(tr, m)
    r0 = row0 // tr
    rs = pl.BlockSpec((tr, GW), lambda i: (i, 0))
    ps = pl.BlockSpec((None, 1, GW), lambda i: (l, 0, 0))
    return _into(
        dst, _rwkv_post_kernel,
        grid=(m // tr,),
        in_specs=[rs, rs, pl.BlockSpec((tr, GW), lambda i: (r0 + i, 0)),
                  pl.BlockSpec((tr, GW), lambda i: (r0 + i, 2)), rs, rs, rs, ps, ps, ps],
        out_specs=pl.BlockSpec((tr, GW), lambda i: (r0 + i, 0)),
        compiler_params=_cp("arbitrary"),
        name="rwkv_post",
    )(of, ob, y, y, kdf, kdb, g, prm["ln_w"], prm["ln_b"], prm["r_k"])


def _rwkv_mixer(y, row0, nb, t, prm, l, s0f, s0b, dst):
    m = nb * t
    kk, wf, wb, bf, bb, kdf, kdb, g = _rwkv_prep(y, row0, m, prm, l)
    of, ob, sf, sb = _rwkv_scan(y, row0, nb, t, (kk, wf, wb, bf, bb, kdf, kdb), s0f, s0b)
    ya = _rwkv_post(of, ob, y, row0, m, kdf, kdb, g, prm, l, dst)
    return ya, sf, sb


def _state_in(s):
    b, h, v, k = s.shape
    return s.reshape(b, h // 2, 2, v, k).transpose(0, 1, 3, 2, 4).reshape(b, h // 2 * v, 2 * k)


def _state_out(s):
    b = s.shape[0]
    hp = s.shape[1] // HD_A
    return s.reshape(b, hp, HD_A, 2, HD_A).transpose(0, 1, 3, 2, 4).reshape(b, 2 * hp, HD_A, HD_A)


def _rope_tables(t):
    pos = jnp.arange(t)
    row = (pos // GRID_W).astype(F32)
    col = (pos % GRID_W).astype(F32)
    lane = jnp.arange(LANES)
    dd = lane % DK_B
    half = DK_B // 2
    nf = half // 2
    part = dd // half
    ii = dd % half
    inv = ROPE_BASE ** (-jnp.arange(nf, dtype=F32) / nf)
    p = jnp.where(part[None, :] == 0, row[:, None], col[:, None])
    ang = p * inv[ii % nf][None, :]
    cos, sin = jnp.cos(ang), jnp.sin(ang)
    first = (ii < nf)[None, :]
    return cos, jnp.where(first, -sin, 0.0), jnp.where(first, 0.0, sin), nf


def _rope(x, cos, s1, s2, nf):
    return x * cos + pltpu.roll(x, LANES - nf, 1) * s1 + pltpu.roll(x, nf, 1) * s2


def _split_heads(q, width):
    lane = lax.broadcasted_iota(jnp.int32, q.shape, 1)
    return jnp.concatenate([jnp.where(lane < width, q, 0.0), jnp.where(lane >= width, q, 0.0)],
                           axis=0).astype(BF16)


def _lam(lam_ref, lam_init):
    lv = lam_ref[...]
    a = jnp.sum(lv[0:1] * lv[1:2], axis=-1, keepdims=True)
    b = jnp.sum(lv[2:3] * lv[3:4], axis=-1, keepdims=True)
    return jnp.exp(a) - jnp.exp(b) + lam_init


def _diff_ctx_kernel(q_ref, k_ref, v_ref, lam_ref, g_ref, o_ref, ko_ref, vo_ref, *, lam_init):
    t = q_ref.shape[0]
    lam = _lam(lam_ref, lam_init)
    for hh in range(q_ref.shape[1] // LANES):
        cols = slice(hh * LANES, (hh + 1) * LANES)
        k = k_ref[:, cols]
        v = v_ref[:, cols]
        ko_ref[hh] = k
        vo_ref[hh] = v
        s = _dot_nt(_split_heads(q_ref[:, cols] * (DK_B ** -0.5), DK_B), k.astype(BF16))
        e = jnp.exp(s - jnp.max(s, axis=-1, keepdims=True))
        o2 = _dot(e.astype(BF16), v.astype(BF16)) * (1.0 / jnp.sum(e, axis=-1, keepdims=True))
        o = o2[:t] - lam * o2[t:]
        ms = jnp.mean(o * o, axis=-1, keepdims=True)
        o_ref[:, cols] = (o * lax.rsqrt(ms + EPS) * g_ref[...] * (1.0 - lam_init)).astype(o_ref.dtype)


def _diff_ctx(y, row0, nb, t, lam_all, g_all, l, lam_init, col0, dst, dst_k, dst_v, hps=4):
    nh = GW // LANES
    rb = row0 // t
    wb = hps * LANES
    cq, ck, cv = col0 // wb, (col0 + GW) // wb, (col0 + 2 * GW) // wb
    ysp = lambda cb: pl.BlockSpec((t, wb), lambda b, h: (rb + b, cb + h))
    cspec = pl.BlockSpec((None, None, hps, t, LANES), lambda b, h: (b, l, h, 0, 0))
    return _into(
        [dst, dst_k, dst_v], functools.partial(_diff_ctx_kernel, lam_init=lam_init),
        grid=(nb, nh // hps),
        in_specs=[ysp(cq), ysp(ck), ysp(cv),
                  pl.BlockSpec((None, 4, DK_B), lambda b, h: (l, 0, 0)),
                  pl.BlockSpec((None, 1, LANES), lambda b, h: (l, 0, 0))],
        out_specs=[pl.BlockSpec((t, wb), lambda b, h: (rb + b, h)), cspec, cspec],
        compiler_params=_cp("arbitrary", "arbitrary"),
        name="diff_ctx",
    )(y, y, y, lam_all, g_all)


def _diff_lat_kernel(q_ref, k_ref, v_ref, ck_ref, cv_ref, cos_ref, s1_ref, s2_ref, lam_ref, g_ref,
                     o_ref, kb_ref, vb_ref, *, lam_init, tq, nf):
    t = q_ref.shape[0]
    cos, s1, s2 = cos_ref[...], s1_ref[...], s2_ref[...]
    kb_ref[...] = _rope(k_ref[...], cos, s1, s2, nf).astype(BF16)
    vb_ref[...] = v_ref[...].astype(BF16)
    ckb = ck_ref[...].astype(BF16)
    cvb = cv_ref[...].astype(BF16)
    lam = _lam(lam_ref, lam_init)
    gsc = g_ref[...] * (1.0 - lam_init)

    def blk(i, carry):
        rows = pl.ds(pl.multiple_of(i * tq, tq), tq)
        q = _rope(q_ref[rows, :] * (DK_B ** -0.5), cos_ref[rows, :], s1_ref[rows, :], s2_ref[rows, :], nf)
        q2 = _split_heads(q, DK_B)
        sc = _dot_nt(q2, ckb)
        sl = _dot_nt(q2, kb_ref[...])
        mx = jnp.maximum(jnp.max(sc, axis=-1, keepdims=True), jnp.max(sl, axis=-1, keepdims=True))
        ec = jnp.exp(sc - mx)
        el = jnp.exp(sl - mx)
        rz = 1.0 / (jnp.sum(ec, axis=-1, keepdims=True) + jnp.sum(el, axis=-1, keepdims=True))
        o2 = (_dot(ec.astype(BF16), cvb) + _dot(el.astype(BF16), vb_ref[...])) * rz
        o = o2[:tq] - lam * o2[tq:]
        ms = jnp.mean(o * o, axis=-1, keepdims=True)
        o_ref[rows, :] = (o * lax.rsqrt(ms + EPS) * gsc).astype(o_ref.dtype)
        return carry

    lax.fori_loop(0, t // tq, blk, 0)


def _diff_lat(y, row0, nb, t, cache_k, cache_v, lam_all, g_all, l, lam_init, col0, dst, tq=256):
    nh = GW // LANES
    rb = row0 // t
    tq = min(tq, t)
    cq, ck, cv = col0 // LANES, (col0 + GW) // LANES, (col0 + 2 * GW) // LANES
    ysp = lambda cb: pl.BlockSpec((t, LANES), lambda b, h: (rb + b, cb + h))
    past = cache_k.shape[3]
    csp = pl.BlockSpec((None, None, None, past, LANES), lambda b, h: (b, l, h, 0, 0))
    cos, s1, s2, nf = _rope_tables(t)
    tsp = pl.BlockSpec((t, LANES), lambda b, h: (0, 0))
    return _into(
        dst, functools.partial(_diff_lat_kernel, lam_init=lam_init, tq=tq, nf=nf),
        grid=(nb, nh),
        in_specs=[ysp(cq), ysp(ck), ysp(cv), csp, csp, tsp, tsp, tsp,
                  pl.BlockSpec((None, 4, DK_B), lambda b, h: (l, 0, 0)),
                  pl.BlockSpec((None, 1, LANES), lambda b, h: (l, 0, 0))],
        out_specs=pl.BlockSpec((t, LANES), lambda b, h: (rb + b, h)),
        scratch_shapes=[pltpu.VMEM((t, LANES), BF16), pltpu.VMEM((t, LANES), BF16)],
        compiler_params=_cp("arbitrary", "arbitrary"),
        name="diff_lat",
    )(y, y, y, cache_k, cache_v, cos, s1, s2, lam_all, g_all)


def _na_ctx_kernel(q_ref, k_ref, v_ref, o_ref, ko_ref, vo_ref):
    t = q_ref.shape[0]
    lane = lax.broadcasted_iota(jnp.int32, (t, LANES), 1)
    for pp in range(q_ref.shape[1] // LANES):
        cols = slice(pp * LANES, (pp + 1) * LANES)
        k = k_ref[:, cols]
        v = v_ref[:, cols]
        for h in range(2):
            ko_ref[2 * pp + h] = k[:, h * HD_C:(h + 1) * HD_C]
            vo_ref[2 * pp + h] = v[:, h * HD_C:(h + 1) * HD_C]
        s = _dot_nt(_split_heads(q_ref[:, cols] * (HD_C ** -0.5), HD_C), k.astype(BF16))
        e = jnp.exp(s - jnp.max(s, axis=-1, keepdims=True))
        o2 = _dot(e.astype(BF16), v.astype(BF16)) * (1.0 / jnp.sum(e, axis=-1, keepdims=True))
        o_ref[:, cols] = jnp.where(lane < HD_C, o2[:t], o2[t:]).astype(o_ref.dtype)


def _na_ctx(y, row0, nb, t, l, col0, dst, dst_k, dst_v, pps=4):
    npair = GW // LANES
    rb = row0 // t
    wb = pps * LANES
    cq, ck, cv = col0 // wb, (col0 + GW) // wb, (col0 + 2 * GW) // wb
    ysp = lambda cb: pl.BlockSpec((t, wb), lambda b, p: (rb + b, cb + p))
    cspec = pl.BlockSpec((None, None, 2 * pps, t, HD_C), lambda b, p: (b, l, p, 0, 0))
    return _into(
        [dst, dst_k, dst_v], _na_ctx_kernel,
        grid=(nb, npair // pps),
        in_specs=[ysp(cq), ysp(ck), ysp(cv)],
        out_specs=[pl.BlockSpec((t, wb), lambda b, p: (rb + b, p)), cspec, cspec],
        compiler_params=_cp("arbitrary", "arbitrary"),
        name="na_ctx",
    )(y, y, y)


def _na_bias_kernel(rpb_ref, o_ref, t_ref):
    ww = lax.broadcasted_iota(jnp.int32, (GRID_W, GRID_W), 0)
    jj = lax.broadcasted_iota(jnp.int32, (GRID_W, GRID_W), 1)
    c0 = jnp.clip(ww - WIN_C // 2, 0, GRID_W - WIN_C)
    allowed = (jj >= c0) & (jj < c0 + WIN_C)
    diff = jj - ww + (WIN_C - 1)
    for dr in range(2 * WIN_R - 1):
        acc = jnp.full((GRID_W, GRID_W), NEG, F32)
        for dc in range(2 * WIN_C - 1):
            acc = jnp.where(diff == dc, rpb_ref[dr, dc], acc)
        t_ref[dr] = jnp.where(allowed, acc, NEG)
    for pat in range(WIN_R):
        for i in range(WIN_R):
            o_ref[pat, :, i * GRID_W:(i + 1) * GRID_W] = t_ref[i - pat + WIN_R - 1]


def _na_bias_table(rpb):
    nl, nh, ndr, ndc = rpb.shape
    return pl.pallas_call(
        _na_bias_kernel,
        grid=(nl, nh),
        in_specs=[pl.BlockSpec((None, None, ndr, ndc), lambda l, h: (l, h, 0, 0), memory_space=pltpu.SMEM)],
        out_specs=pl.BlockSpec((None, None, WIN_R, GRID_W, WIN_R * GRID_W), lambda l, h: (l, h // 2, 0, h % 2, 0)),
        out_shape=jax.ShapeDtypeStruct((nl, nh // 2, WIN_R, 2 * GRID_W, WIN_R * GRID_W), F32),
        scratch_shapes=[pltpu.VMEM((ndr, GRID_W, GRID_W), F32)],
        compiler_params=_cp("arbitrary", "arbitrary"),
        name="na_bias",
    )(rpb)


def _na_lat_kernel(q_ref, k_ref, v_ref, ck_ref, cv_ref, bias_ref, o_ref, kb_ref, vb_ref):
    t = q_ref.shape[0]
    rows = t // GRID_W
    kr = min(WIN_R, rows)
    kb_ref[...] = k_ref[...].astype(BF16)
    vb_ref[...] = v_ref[...].astype(BF16)
    ckb = ck_ref[...].astype(BF16)
    cvb = cv_ref[...].astype(BF16)
    lane = lax.broadcasted_iota(jnp.int32, (GRID_W, LANES), 1)

    def row_step(r, carry):
        r0 = jnp.clip(r - kr // 2, 0, rows - kr)
        pat = r - r0
        qrows = pl.ds(pl.multiple_of(r * GRID_W, GRID_W), GRID_W)
        wrows = pl.ds(pl.multiple_of(r0 * GRID_W, GRID_W), kr * GRID_W)
        q2 = _split_heads(q_ref[qrows, :] * (HD_C ** -0.5), HD_C)
        sw = _dot_nt(q2, kb_ref[wrows, :]) + bias_ref[pat]
        sc = _dot_nt(q2, ckb)
        mx = jnp.maximum(jnp.max(sw, axis=-1, keepdims=True), jnp.max(sc, axis=-1, keepdims=True))
        ew = jnp.exp(sw - mx)
        ec = jnp.exp(sc - mx)
        rz = 1.0 / (jnp.sum(ew, axis=-1, keepdims=True) + jnp.sum(ec, axis=-1, keepdims=True))
        o2 = (_dot(ew.astype(BF16), vb_ref[wrows, :]) + _dot(ec.astype(BF16), cvb)) * rz
        o_ref[qrows, :] = jnp.where(lane < HD_C, o2[:GRID_W], o2[GRID_W:]).astype(o_ref.dtype)
        return carry

    lax.fori_loop(0, rows, row_step, 0, unroll=2)


def _na_lat(y, row0, nb, t, ctx_k, ctx_v, bias, l, col0, dst):
    npair = GW // LANES
    rb = row0 // t
    cq, ck, cv = col0 // LANES, (col0 + GW) // LANES, (col0 + 2 * GW) // LANES
    ysp = lambda cb: pl.BlockSpec((t, LANES), lambda b, p: (rb + b, cb + p))
    past = ctx_k.shape[3]
    csp = pl.BlockSpec((None, None, None, past, LANES), lambda b, p: (b, l, p, 0, 0))
    bsp = pl.BlockSpec((None, None, WIN_R, 2 * GRID_W, WIN_R * GRID_W), lambda b, p: (l, p, 0, 0, 0))
    return _into(
        dst, _na_lat_kernel,
        grid=(nb, npair),
        in_specs=[ysp(cq), ysp(ck), ysp(cv), csp, csp, bsp],
        out_specs=pl.BlockSpec((t, LANES), lambda b, p: (rb + b, p)),
        scratch_shapes=[pltpu.VMEM((t, LANES), BF16), pltpu.VMEM((t, LANES), BF16)],
        compiler_params=_cp("arbitrary", "arbitrary"),
        name="na_lat",
    )(y, y, y, ctx_k, ctx_v, bias)


def _hy_positions(length):
    t = jnp.linspace(0.0, 1.0, length, dtype=F32)[:, None]
    ang = 2.0 * math.pi * jnp.arange(length, dtype=F32) / length
    bands = jnp.linspace(1e-4, HY_BANDS - 1, HY_BANDS, dtype=F32)
    z = jnp.concatenate([t, jnp.cos(ang[:, None] * bands[None, :]),
                         -jnp.sin(ang[:, None] * bands[None, :])], axis=-1)
    z = jnp.pad(z, ((0, 0), (0, LANES - z.shape[1])))
    mn = math.log(HY_TARGET) / HY_SLOW
    mx = math.log(HY_TARGET) / HY_FAST
    deltas = jnp.abs(jnp.linspace(mn, mx, GW, dtype=F32))
    decay = jnp.exp(-t * deltas[None, :])
    return z, decay


def _hy_filter_kernel(z_ref, dec_ref, w1_ref, b1_ref, f1_ref, w2_ref, b2_ref, f2_ref, w3_ref, o_ref):
    hid = jnp.sin(f1_ref[...] * (_dot3(z_ref[...], w1_ref[...]) + b1_ref[...]))
    hid = jnp.sin(f2_ref[...] * (_dot3(hid, w2_ref[...]) + b2_ref[...]))
    h = _dot3(hid, w3_ref[...])
    dec = dec_ref[...]
    first = (lax.broadcasted_iota(jnp.int32, dec.shape, 0) + pl.program_id(0) * dec.shape[0]) == 0
    for o in range(h.shape[1] // (2 * GW)):
        hf = h[:, (2 * o) * GW:(2 * o + 1) * GW] * dec
        hb = jnp.where(first, 0.0, h[:, (2 * o + 1) * GW:(2 * o + 2) * GW] * dec)
        o_ref[:, (2 * o) * GW:(2 * o + 1) * GW] = hf + hb
        o_ref[:, (2 * o + 1) * GW:(2 * o + 2) * GW] = hf - hb


def _hy_filters(length, prm, l):
    z, decay = _hy_positions(length)
    n3 = prm["w3"].shape[2]
    hid = prm["w2"].shape[1]
    tl = min(length, 256)
    ps = lambda shape: pl.BlockSpec((None,) + shape, lambda i: (l,) + (0,) * len(shape))
    return pl.pallas_call(
        _hy_filter_kernel,
        grid=(length // tl,),
        in_specs=[pl.BlockSpec((tl, LANES), lambda i: (i, 0)), pl.BlockSpec((tl, GW), lambda i: (i, 0)),
                  ps((LANES, hid)), ps((1, hid)), ps((1, hid)), ps((hid, hid)), ps((1, hid)), ps((1, hid)),
                  ps((hid, n3))],
        out_specs=pl.BlockSpec((tl, n3), lambda i: (i, 0)),
        out_shape=jax.ShapeDtypeStruct((length, n3), F32),
        compiler_params=_cp("arbitrary"),
        name="hy_filter",
    )(z, decay, prm["w1p"], prm["b1"], prm["f1"], prm["w2"], prm["b2"], prm["f2"], prm["w3"])


def _dft_mats(length):
    n2 = 2 * length
    k = jnp.arange(length, dtype=jnp.int32)
    kn = (k[:, None] * k[None, :]) % n2
    ang = kn.astype(F32) * (2.0 * math.pi / n2)
    c, s = jnp.cos(ang), jnp.sin(ang)
    alt = jnp.where(k % 2 == 0, 1.0, -1.0).astype(F32)
    f_im = jnp.where((k == 0)[:, None], alt[None, :], -s)
    fwd = jnp.concatenate([c, f_im], axis=0).astype(BF16)
    ck = jnp.where(k == 0, 1.0, 2.0)[None, :]
    g_re = c * ck / n2
    g_im = jnp.where((k == 0)[None, :], alt[:, None] / n2, -2.0 * s / n2)
    inv = jnp.concatenate([g_re, g_im], axis=1).astype(BF16)
    return fwd, inv


def _mm_plain_kernel(a_ref, w_ref, o_ref):
    o_ref[...] = _dot(a_ref[...], w_ref[...].astype(BF16))


def _hy_spectrum(fwd, hk, tm=512, tn=512):
    m, k = fwd.shape
    n = hk.shape[1]
    tm, tn = min(tm, m), min(tn, n)
    return pl.pallas_call(
        _mm_plain_kernel,
        grid=(m // tm, n // tn),
        in_specs=[pl.BlockSpec((tm, k), lambda i, j: (i, 0)), pl.BlockSpec((k, tn), lambda i, j: (0, j))],
        out_specs=pl.BlockSpec((tm, tn), lambda i, j: (i, j)),
        out_shape=jax.ShapeDtypeStruct((m, n), F32),
        compiler_params=_cp("arbitrary", "arbitrary"),
        name="hy_spectrum",
    )(fwd, hk)


def _short_conv_kernel(u_ref, w_ref, b_ref, o_ref):
    u = u_ref[...]
    t = u.shape[0]
    rowi = lax.broadcasted_iota(jnp.int32, u.shape, 0)
    prev = jnp.where(rowi == 0, 0.0, pltpu.roll(u, 1, 0))
    nxt = jnp.where(rowi == t - 1, 0.0, pltpu.roll(u, t - 1, 0))
    o_ref[...] = prev * w_ref[0:1, :] + u * w_ref[1:2, :] + nxt * w_ref[2:3, :] + b_ref[...]


def _short_conv(y, row0, nb, t, w_all, b_all, l, col0, width, tc=512):
    rb = row0 // t
    cb = col0 // tc
    return pl.pallas_call(
        _short_conv_kernel,
        grid=(nb, width // tc),
        in_specs=[pl.BlockSpec((t, tc), lambda b, j: (rb + b, cb + j)),
                  pl.BlockSpec((None, 3, tc), lambda b, j: (l, 0, j)),
                  pl.BlockSpec((None, 1, tc), lambda b, j: (l, 0, j))],
        out_specs=pl.BlockSpec((t, tc), lambda b, j: (b, j)),
        out_shape=jax.ShapeDtypeStruct((nb * t, width), F32),
        compiler_params=_cp("arbitrary", "arbitrary"),
        name="short_conv",
    )(y, w_all, b_all)


def _hy_fwd_kernel(fre_ref, fim_ref, z_ref, kre_ref, kny_ref, kim_ref, o_ref):
    zb = z_ref[...].astype(BF16)
    re = _dot(fre_ref[...], zb)
    im = _dot(fim_ref[...], zb)
    tm = re.shape[0]
    dc = (lax.broadcasted_iota(jnp.int32, re.shape, 0) + pl.program_id(1) * tm) == 0
    kre = kre_ref[...]
    kim = jnp.where(dc, kny_ref[...], kim_ref[...])
    o_ref[0] = jnp.where(dc, re * kre, re * kre - im * kim).astype(o_ref.dtype)
    o_ref[1] = jnp.where(dc, im * kim, re * kim + im * kre).astype(o_ref.dtype)


def _hy_fwd(fwd, z, zcol, nb, t, kspec, o, tm=512, tn=512):
    tm, tn = min(tm, t), min(tn, GW)
    ni = t // tm
    zc = zcol // tn
    cs, cd = (2 * o) * GW // tn, (2 * o + 1) * GW // tn
    return pl.pallas_call(
        _hy_fwd_kernel,
        grid=(nb, ni, GW // tn),
        in_specs=[pl.BlockSpec((tm, t), lambda b, i, j: (i, 0)),
                  pl.BlockSpec((tm, t), lambda b, i, j: (ni + i, 0)),
                  pl.BlockSpec((t, tn), lambda b, i, j: (b, zc + j)),
                  pl.BlockSpec((tm, tn), lambda b, i, j: (i, cs + j)),
                  pl.BlockSpec((tm, tn), lambda b, i, j: (ni + i, cs + j)),
                  pl.BlockSpec((tm, tn), lambda b, i, j: (ni + i, cd + j))],
        out_specs=pl.BlockSpec((None, 2, tm, tn), lambda b, i, j: (b, 0, i, j)),
        out_shape=jax.ShapeDtypeStruct((nb, 2, t, GW), BF16),
        compiler_params=_cp("arbitrary", "arbitrary", "arbitrary"),
        name="hy_fwd",
    )(fwd, fwd, z, kspec, kspec, kspec)


def _hy_inv_kernel(g_ref, s_ref, zin_ref, gate_ref, skip_ref, o_ref):
    yv = _dot(g_ref[...], s_ref[...])
    o_ref[...] = (gate_ref[...] * (yv + skip_ref[...] * zin_ref[...])).astype(o_ref.dtype)


def _hy_inv(inv, spec, z, zcol, gate, gcol, skip_all, l, o, nb, t, dst, row0=0, tm=512, tn=512):
    tm, tn = min(tm, t), min(tn, GW)
    ni = t // tm
    zc, gc = zcol // tn, gcol // tn
    ro = row0 // tm
    return _into(
        dst, _hy_inv_kernel,
        grid=(nb, ni, GW // tn),
        in_specs=[pl.BlockSpec((tm, 2 * t), lambda b, i, j: (i, 0)),
                  pl.BlockSpec((None, 2 * t, tn), lambda b, i, j: (b, 0, j)),
                  pl.BlockSpec((tm, tn), lambda b, i, j: (b * ni + i, zc + j)),
                  pl.BlockSpec((tm, tn), lambda b, i, j: (b * ni + i, gc + j)),
                  pl.BlockSpec((None, None, 1, tn), lambda b, i, j: (l, o, 0, j))],
        out_specs=pl.BlockSpec((tm, tn), lambda b, i, j: (ro + b * ni + i, j)),
        compiler_params=_cp("arbitrary", "arbitrary", "arbitrary"),
        name="hy_inv",
    )(inv, spec.reshape(nb, 2 * t, GW), z, gate, skip_all)


def _hyena_mixer(y, row0, nb, t, prm, l, col0, consts, dst):
    fwd, inv, kspec = consts
    u = _short_conv(y, row0, nb, t, prm["sconv_w"], prm["sconv_b"], l, col0, 3 * GW)
    spec = _hy_fwd(fwd, u, 0, nb, t, kspec, 0)
    z = _hy_inv(inv, spec, u, 0, u, GW, prm["skip"], l, 0, nb, t, jax.ShapeDtypeStruct((nb * t, GW), F32))
    spec = _hy_fwd(fwd, z, 0, nb, t, kspec, 1)
    return _hy_inv(inv, spec, z, 0, u, 2 * GW, prm["skip"], l, 1, nb, t, dst, row0)


def _hyena_consts(t, prm, l):
    fwd, inv = _dft_mats(t)
    return fwd, inv, _hy_spectrum(fwd, _hy_filters(t, prm, l))


def _permute_w_in(w_in):
    gw = GW
    big = w_in[..., :3 * gw]
    o = 3 * gw
    small = w_in[..., o:o + 416]
    rest = w_in[..., o + 416:]
    pad = jnp.zeros(w_in.shape[:-1] + (R_SMALL - 416,), w_in.dtype)
    return jnp.concatenate([big, rest, small, pad], axis=-1).astype(BF16)


def _pad_rows(w, rows, at=0):
    pad = [(0, 0)] * (w.ndim - 2) + [(at, rows - at - w.shape[-2]), (0, 0)]
    return jnp.pad(w, pad)


def kernel(x_prompt, x_sample, cache_diff_k, cache_diff_v, cache_na_k, cache_na_v, state_rwkv_fwd,
           state_rwkv_bwd, c, c_ctx, norm1_g, norm2_g, w_mod, b_mod, w_in, rwkv_w0, rwkv_w2, rwkv_a0,
           rwkv_a2, rwkv_g2, rwkv_k_k, rwkv_k_a, rwkv_r_k, rwkv_ln_w, rwkv_ln_b, diff_lam,
           diff_subln_g, na_rpb, hy_sconv_w, hy_sconv_b, hy_f_w1, hy_f_b1, hy_f_freq1, hy_f_w2,
           hy_f_b2, hy_f_freq2, hy_f_w3, hy_skip, w_out, w_up, w_down, final_g):
    bp, tp, d = x_prompt.shape
    bs, ts, _ = x_sample.shape
    nl = w_in.shape[0]
    mp, ms = bp * tp, bs * ts
    assert mp % 1024 == 0 and ts % 1024 == 0 and bs + 1 <= 8
    grp = _group_fn(mp, ts)

    mt = mp + ms
    xp, xs = x_prompt.reshape(mp, d), x_sample.reshape(ms, d)
    cv = jnp.concatenate([c_ctx[None], c, jnp.zeros((8 - 1 - bs, d), F32)], axis=0)
    mod = _modulation(cv, w_mod, b_mod).reshape(nl, 8, N_MOD, 1, d)

    w_in_b = _permute_w_in(w_in)
    w_out_b = w_out.astype(BF16)
    w_up_b = w_up.astype(BF16)
    w_down_b = w_down.astype(BF16)
    row = lambda p: p.reshape(nl, 1, -1)
    rw = {
        "w0": rwkv_w0, "a0": rwkv_a0,
        "w2p": jnp.stack([_pad_rows(rwkv_w2[:, 0], LANES, 0), _pad_rows(rwkv_w2[:, 1], LANES, 64)], axis=1),
        "a2p": jnp.stack([_pad_rows(rwkv_a2[:, 0], LANES, 0), _pad_rows(rwkv_a2[:, 1], LANES, 64)], axis=1),
        "g2p": _pad_rows(rwkv_g2, 2 * LANES, 0),
        "k_k": row(rwkv_k_k), "k_a": row(rwkv_k_a), "r_k": row(rwkv_r_k),
        "ln_w": row(rwkv_ln_w), "ln_b": row(rwkv_ln_b),
    }
    hy = {
        "sconv_w": hy_sconv_w, "sconv_b": hy_sconv_b.reshape(nl, 1, -1),
        "w1p": _pad_rows(hy_f_w1, LANES, 0), "b1": row(hy_f_b1), "f1": row(hy_f_freq1),
        "w2": hy_f_w2, "b2": row(hy_f_b2), "f2": row(hy_f_freq2), "w3": hy_f_w3,
        "skip": hy_skip.reshape(nl, 2, 1, GW),
    }
    subln = diff_subln_g.reshape(nl, 1, -1)
    norm1 = norm1_g.reshape(nl, 1, d)
    norm2 = norm2_g.reshape(nl, 1, d)
    na_bias = _na_bias_table(na_rpb)
    pair = lambda a: a.reshape(a.shape[0], a.shape[1], a.shape[2] // 2, 2, a.shape[3], a.shape[4]) \
        .transpose(0, 1, 2, 4, 3, 5).reshape(a.shape[0], a.shape[1], a.shape[2] // 2, a.shape[3], 2 * a.shape[4])
    na_ck, na_cv = pair(cache_na_k), pair(cache_na_v)
    zero_state = jnp.zeros((bp, GW // LANES * HD_A, LANES), F32)

    c_b, c_c, c_d = 3 * GW, 6 * GW, 9 * GW
    act = lambda dtype, width=d: jax.ShapeDtypeStruct((mt, width), dtype)
    xsrc = ((xp, 0, 0, mp), (xs, 0, mp, ms))
    dk = jax.ShapeDtypeStruct((bp, nl, GW // LANES, tp, LANES), F32)
    nk = jax.ShapeDtypeStruct((bp, nl, GW // HD_C, tp, HD_C), F32)
    dv, nv = dk, nk
    states = [[], []]
    for l in range(nl):
        lam_init = 0.8 - 0.6 * math.exp(-0.3 * l)
        h = act(BF16)
        for xa, xrow0, orow0, m in xsrc:
            h = _norm_mod(xa, xrow0, m, norm1, mod, l, 1, 0, grp, h, orow0)
        y = _mm_fullk([h], w_in_b, l, act(F32, w_in_b.shape[2]), 0, mt, name="proj_in")

        ya, sf, sb = _rwkv_mixer(y, 0, bp, tp, rw, l, zero_state, zero_state, act(BF16, GW))
        yb, dk, dv = _diff_ctx(y, 0, bp, tp, diff_lam, subln, l, lam_init, c_b, act(BF16, GW), dk, dv)
        yc, nk, nv = _na_ctx(y, 0, bp, tp, l, c_c, act(BF16, GW), nk, nv)
        yd = _hyena_mixer(y, 0, bp, tp, hy, l, c_d, _hyena_consts(tp, hy, l), act(BF16, GW))
        ya, _, _ = _rwkv_mixer(y, mp, bs, ts, rw, l, _state_in(state_rwkv_fwd[:, l]),
                               _state_in(state_rwkv_bwd[:, l]), ya)
        yb = _diff_lat(y, mp, bs, ts, cache_diff_k, cache_diff_v, diff_lam, subln, l, lam_init, c_b, yb)
        yc = _na_lat(y, mp, bs, ts, na_ck, na_cv, na_bias, l, c_c, yc)
        yd = _hyena_mixer(y, mp, bs, ts, hy, l, c_d, _hyena_consts(ts, hy, l), yd)

        x = act(F32)
        for xa, xrow0, orow0, m in xsrc:
            x = _mm_fullk([ya, yb, yc, yd], w_out_b, l, x, orow0, m, epilogue="resid",
                          resid=(xa, xrow0, mod, 2, grp), name="proj_out")
        h = _norm_mod(x, 0, mt, norm2, mod, l, 4, 3, grp, act(BF16), 0)
        u = _mm_fullk([h], w_up_b, l, act(BF16, w_up_b.shape[2]), 0, mt, epilogue="relu2", name="ffn_up")
        x = _mm_kloop_resid(u, w_down_b, l, x, mod, 5, grp)
        xsrc = ((x, 0, 0, mt),)
        states[0].append(_state_out(sf))
        states[1].append(_state_out(sb))

    return (_rmsnorm(x, 0, mp, final_g).reshape(bp, tp, d), _rmsnorm(x, mp, ms, final_g).reshape(bs, ts, d),
            dk, dv, nk, nv, jnp.stack(states[0], axis=1), jnp.stack(states[1], axis=1))
```
